```python
import jax
import jax.numpy as jnp
from jax import lax
import numpy as np

D_MODEL = 4096
BATCH = 8
SEQ = 2048
DEPTH = 2

GRID_W = 64
CTX_LEN = 256
N_BRANCH = 4
BRANCH_W = D_MODEL // 4
HGRN_DK = 128
HGRN_DV = 128
HGRN_HEADS = BRANCH_W // HGRN_DV
HGRN_CHUNK = 64
HEAD_DIM = 128
ATTN_Q_HEADS = BRANCH_W // HEAD_DIM
ATTN_KV_HEADS = ATTN_Q_HEADS // 4
ROPE_AXIS_DIM = HEAD_DIM // 2
ROPE_THETA = 10000.0
Q_BLOCK = 128
SHORT_CONV_W = 3
CONF_CONV_W = 31
FFN_DIM = D_MODEL
FFN_RESIDUAL = 0.5
N_MOD = 9
EPS = 1e-6
LB_FLOOR = 1e-30

HK = HGRN_HEADS * HGRN_DK
HV = HGRN_HEADS * HGRN_DV
AQ = ATTN_Q_HEADS * HEAD_DIM
AKV = ATTN_KV_HEADS * HEAD_DIM
OFF_F_FWD = 0
OFF_F_BWD = OFF_F_FWD + HK
OFF_I = OFF_F_BWD + HK
OFF_K = OFF_I + HV
OFF_V = OFF_K + AKV
CTX_SIDE_COLS = OFF_V + AKV
OFF_HQ = CTX_SIDE_COLS
OFF_HG = OFF_HQ + HK
OFF_AQ = OFF_HG + HV
OFF_SC = OFF_AQ + AQ
OFF_GLU = OFF_SC + 3 * BRANCH_W
OFF_GATE = OFF_GLU + 2 * BRANCH_W
N_IN_COLS = OFF_GATE + N_BRANCH * D_MODEL

kernel_name = 'hybrid_diffusion_trunk'


def rms_norm(x, g):
    xf = x.astype(jnp.float32)
    y = xf * lax.rsqrt(jnp.mean(xf * xf, axis=-1, keepdims=True) + EPS)
    return (y * g.astype(jnp.float32)).astype(x.dtype)


def layer_norm(x, g, b):
    xf = x.astype(jnp.float32)
    xc = xf - jnp.mean(xf, axis=-1, keepdims=True)
    y = xc * lax.rsqrt(jnp.mean(xc * xc, axis=-1, keepdims=True) + EPS)
    return (y * g.astype(jnp.float32) + b.astype(jnp.float32)).astype(x.dtype)


def modulate(h, shift, scale):
    return h * (1 + scale) + shift


def depthwise_conv(u, w):
    width = w.shape[0]
    return lax.conv_general_dilated(u, w.astype(u.dtype)[:, None, :], window_strides=(1,),
                                    padding=[(width // 2, width // 2)],
                                    dimension_numbers=('NWC', 'WIO', 'NWC'),
                                    feature_group_count=u.shape[-1])


def swiglu(h, w1, w2):
    a, b = jnp.split(h @ w1, 2, axis=-1)
    return (jax.nn.silu(a) * b) @ w2


def half_ffn(s, shift, scale, gate, g_pre, g_post, w1, w2):
    y = swiglu(modulate(rms_norm(s, g_pre), shift, scale), w1, w2)
    return s + FFN_RESIDUAL * gate * rms_norm(y, g_post)


def to_heads(a, n_heads, d):
    return a.reshape(a.shape[0], a.shape[1], n_heads, d)


def flip_t(a):
    return jnp.flip(a, axis=1)


def grid_angles(n_tokens):
    rows = n_tokens // GRID_W
    r, col = jnp.meshgrid(jnp.arange(rows, dtype=jnp.float32), jnp.arange(GRID_W, dtype=jnp.float32), indexing='ij')
    inv = ROPE_THETA ** (-jnp.arange(0, ROPE_AXIS_DIM, 2, dtype=jnp.float32) / ROPE_AXIS_DIM)
    return r.reshape(-1, 1) * inv, col.reshape(-1, 1) * inv


def rope_axis(u, ang):
    u1, u2 = jnp.split(u, 2, axis=-1)
    cos = jnp.cos(ang)[None, :, None, :].astype(u.dtype)
    sin = jnp.sin(ang)[None, :, None, :].astype(u.dtype)
    return jnp.concatenate([u1 * cos - u2 * sin, u1 * sin + u2 * cos], axis=-1)


def rope_2d(u, ang_r, ang_c):
    return jnp.concatenate([rope_axis(u[..., :ROPE_AXIS_DIM], ang_r), rope_axis(u[..., ROPE_AXIS_DIM:], ang_c)], axis=-1)


def gqa_blocked(q, k, v):
    b, t, hq, d = q.shape
    hkv = k.shape[2]
    nb = t // Q_BLOCK
    qb = q.reshape(b, nb, Q_BLOCK, hkv, hq // hkv, d).transpose(1, 0, 2, 3, 4, 5)
    scale = d ** -0.5

    def one_block(qblk):
        s = jnp.einsum('bqhgd,bkhd->bhgqk', qblk, k).astype(jnp.float32) * scale
        p = jax.nn.softmax(s, axis=-1).astype(v.dtype)
        return jnp.einsum('bhgqk,bkhd->bqhgd', p, v)

    o = lax.map(one_block, qb)
    return o.transpose(1, 0, 2, 3, 4, 5).reshape(b, t, hq * d)


def hgrn_lower_bounds(lb_logits):
    p = jax.nn.softmax(lb_logits.astype(jnp.float32), axis=0)
    return jnp.cumsum(p, axis=0) - p[:1]


def hgrn_log_forget(a, lb):
    return jnp.logaddexp(jnp.log(jnp.maximum(lb, LB_FLOOR)), jnp.log1p(-lb) + jax.nn.log_sigmoid(a.astype(jnp.float32)))


def gla_chunked(q, log_f, v, s0):
    b, t, h, _ = q.shape
    dv = v.shape[-1]
    n = t // HGRN_CHUNK
    k = -jnp.expm1(log_f)

    def chunks(a):
        return a.reshape(b, n, HGRN_CHUNK, h, a.shape[-1]).transpose(1, 0, 3, 2, 4)

    lower = jnp.tril(jnp.ones((HGRN_CHUNK, HGRN_CHUNK), dtype=bool))[:, :, None]

    def step(s, blk):
        qb, kb, vb, fb = blk
        cum = jnp.cumsum(fb, axis=2)
        diff = cum[:, :, :, None, :] - cum[:, :, None, :, :]
        rel = jnp.where(lower, jnp.exp(jnp.minimum(diff, 0.0)), 0.0)
        scores = jnp.einsum('bhik,bhjk,bhijk->bhij', qb, kb, rel)
        out = jnp.einsum('bhij,bhjv->bhiv', scores, vb) + jnp.einsum('bhik,bhkv->bhiv', qb * jnp.exp(cum), s)
        last = cum[:, :, -1:, :]
        s = jnp.exp(last[:, :, 0, :, None]) * s + jnp.einsum('bhjk,bhjv->bhkv', kb * jnp.exp(last - cum), vb)
        return s, out

    s_fin, out = lax.scan(step, s0, (chunks(q), chunks(k), chunks(v), chunks(log_f)))
    return out.transpose(1, 0, 3, 2, 4).reshape(b, t, h, dv), s_fin


def gla_final_state(log_f, v):
    cum = jnp.cumsum(log_f, axis=1)
    k = -jnp.expm1(log_f)
    return jnp.einsum('bthk,bthv->bhkv', k * jnp.exp(cum[:, -1:] - cum), v)


def bidir_gla(q, lf_f, lf_b, v, s_f0, s_b0):
    o_f, s_f = gla_chunked(q, lf_f, v, s_f0)
    o_b, s_b = gla_chunked(flip_t(q), flip_t(lf_b), flip_t(v), s_b0)
    return o_f + flip_t(o_b), s_f, s_b


def hgrn_forget_value(z, lb_f, lb_b):
    lf_f = to_heads(hgrn_log_forget(z[..., OFF_F_FWD:OFF_F_BWD], lb_f), HGRN_HEADS, HGRN_DK)
    lf_b = to_heads(hgrn_log_forget(z[..., OFF_F_BWD:OFF_I], lb_b), HGRN_HEADS, HGRN_DK)
    v = to_heads(z[..., OFF_I:OFF_K].astype(jnp.float32), HGRN_HEADS, HGRN_DV)
    return lf_f, lf_b, v


def hgrn_query(z):
    return to_heads(z[..., OFF_HQ:OFF_HG].astype(jnp.float32), HGRN_HEADS, HGRN_DK)


def hgrn_readout(o, g_pre, norm_g):
    y = rms_norm(o, norm_g) * jax.nn.sigmoid(to_heads(g_pre.astype(jnp.float32), HGRN_HEADS, HGRN_DV))
    return y.reshape(o.shape[0], o.shape[1], HV).astype(g_pre.dtype)


def short_conv_branch(z, w):
    bg, cg, u = jnp.split(z[..., OFF_SC:OFF_GLU], 3, axis=-1)
    return bg * depthwise_conv(cg * u, w)


def conformer_conv_branch(z, dw_w, dw_b, ln_g, ln_b):
    a, gt = jnp.split(z[..., OFF_GLU:OFF_GATE], 2, axis=-1)
    u = depthwise_conv(a * jax.nn.sigmoid(gt), dw_w) + dw_b
    return jax.nn.silu(layer_norm(u, ln_g, ln_b))


def gated_merge(h, branches, w_in, w_branch, w_out):
    acc = None
    for i, br in enumerate(branches):
        lo = OFF_GATE + i * D_MODEL
        term = jax.nn.sigmoid(h @ w_in[:, lo:lo + D_MODEL]) * (br @ w_branch[i])
        acc = term if acc is None else acc + term
    return acc @ w_out


def token_mixers(h_lat, h_ctx, need_ctx, ang_r, ang_c, lb_f, lb_b, w_in, hgrn_norm_g, qk_norm_g,
                 short_conv_w, conf_dw_w, conf_dw_b, conf_ln_g, conf_ln_b, w_branch, w_out):
    z_lat = h_lat @ w_in[:, :OFF_GATE]
    z_ctx = h_ctx @ w_in[:, :(OFF_GATE if need_ctx else CTX_SIDE_COLS)]

    lf_cf, lf_cb, v_c = hgrn_forget_value(z_ctx, lb_f, lb_b)
    lf_lf, lf_lb, v_l = hgrn_forget_value(z_lat, lb_f, lb_b)
    if need_ctx:
        zero = jnp.zeros((h_ctx.shape[0], HGRN_HEADS, HGRN_DK, HGRN_DV), jnp.float32)
        o_c, s_f, s_b = bidir_gla(hgrn_query(z_ctx), lf_cf, lf_cb, v_c, zero, zero)
    else:
        s_f = gla_final_state(lf_cf, v_c)
        s_b = gla_final_state(flip_t(lf_cb), flip_t(v_c))
    o_l, _, _ = bidir_gla(hgrn_query(z_lat), lf_lf, lf_lb, v_l, s_f, s_b)
    hgrn_lat = hgrn_readout(o_l, z_lat[..., OFF_HG:OFF_AQ], hgrn_norm_g)

    k_c = rms_norm(to_heads(z_ctx[..., OFF_K:OFF_V], ATTN_KV_HEADS, HEAD_DIM), qk_norm_g[1])
    va_c = to_heads(z_ctx[..., OFF_V:CTX_SIDE_COLS], ATTN_KV_HEADS, HEAD_DIM)
    q_l = rope_2d(rms_norm(to_heads(z_lat[..., OFF_AQ:OFF_SC], ATTN_Q_HEADS, HEAD_DIM), qk_norm_g[0]), ang_r, ang_c)
    k_l = rope_2d(rms_norm(to_heads(z_lat[..., OFF_K:OFF_V], ATTN_KV_HEADS, HEAD_DIM), qk_norm_g[1]), ang_r, ang_c)
    va_l = to_heads(z_lat[..., OFF_V:CTX_SIDE_COLS], ATTN_KV_HEADS, HEAD_DIM)
    att_lat = gqa_blocked(q_l, jnp.concatenate([k_c, k_l], axis=1), jnp.concatenate([va_c, va_l], axis=1))

    y_lat = gated_merge(h_lat, [hgrn_lat, att_lat, short_conv_branch(z_lat, short_conv_w),
                                conformer_conv_branch(z_lat, conf_dw_w, conf_dw_b, conf_ln_g, conf_ln_b)],
                        w_in, w_branch, w_out)
    if not need_ctx:
        return y_lat, None

    hgrn_ctx = hgrn_readout(o_c, z_ctx[..., OFF_HG:OFF_AQ], hgrn_norm_g)
    q_c = rms_norm(to_heads(z_ctx[..., OFF_AQ:OFF_SC], ATTN_Q_HEADS, HEAD_DIM), qk_norm_g[0])
    att_ctx = gqa_blocked(q_c, k_c, va_c)
    y_ctx = gated_merge(h_ctx, [hgrn_ctx, att_ctx, short_conv_branch(z_ctx, short_conv_w),
                                conformer_conv_branch(z_ctx, conf_dw_w, conf_dw_b, conf_ln_g, conf_ln_b)],
                        w_in, w_branch, w_out)
    return y_lat, y_ctx


def setup_inputs(seed: int = 0) -> dict:
    key = jax.random.key(seed)
    ks = jax.random.split(key, 20)

    def nrm(k, shape, scale):
        return jax.random.normal(k, shape, jnp.float32) * scale

    def gain(k, shape):
        return 1.0 + nrm(k, shape, 0.02)

    return {
        'x': nrm(ks[0], (BATCH, SEQ, D_MODEL), 1.0),
        'c': nrm(ks[1], (BATCH, D_MODEL), 1.0),
        'ctx': nrm(ks[2], (BATCH, CTX_LEN, D_MODEL), 1.0),
        'c_ctx': nrm(ks[3], (D_MODEL,), 1.0),
        'w_mod': nrm(ks[4], (DEPTH, D_MODEL, N_MOD * D_MODEL), 0.5 * D_MODEL ** -0.5),
        'b_mod': nrm(ks[5], (DEPTH, N_MOD * D_MODEL), 0.02),
        'norm_g': gain(ks[6], (DEPTH, 6, D_MODEL)),
        'ffn_w1': nrm(ks[7], (DEPTH, 2, D_MODEL, 2 * FFN_DIM), D_MODEL ** -0.5),
        'ffn_w2': nrm(ks[8], (DEPTH, 2, FFN_DIM, D_MODEL), FFN_DIM ** -0.5),
        'w_in': nrm(ks[9], (DEPTH, D_MODEL, N_IN_COLS), D_MODEL ** -0.5),
        'hgrn_lb_logits': nrm(ks[10], (DEPTH, 2, HK), 1.0),
        'hgrn_norm_g': gain(ks[11], (DEPTH, HGRN_DV)),
        'qk_norm_g': gain(ks[12], (DEPTH, 2, HEAD_DIM)),
        'short_conv_w': nrm(ks[13], (DEPTH, SHORT_CONV_W, BRANCH_W), SHORT_CONV_W ** -0.5),
        'conf_dw_w': nrm(ks[14], (DEPTH, CONF_CONV_W, BRANCH_W), CONF_CONV_W ** -0.5),
        'conf_dw_b': nrm(ks[15], (DEPTH, BRANCH_W), 0.02),
        'conf_ln_g': gain(ks[16], (DEPTH, BRANCH_W)),
        'conf_ln_b': nrm(ks[17], (DEPTH, BRANCH_W), 0.02),
        'w_branch': nrm(ks[18], (DEPTH, N_BRANCH, BRANCH_W, D_MODEL), BRANCH_W ** -0.5),
        'w_out': nrm(ks[19], (DEPTH, D_MODEL, D_MODEL), D_MODEL ** -0.5),
    }


def reference(x, c, ctx, c_ctx, w_mod, b_mod, norm_g, ffn_w1, ffn_w2, w_in, hgrn_lb_logits, hgrn_norm_g,
              qk_norm_g, short_conv_w, conf_dw_w, conf_dw_b, conf_ln_g, conf_ln_b, w_branch, w_out):
    ang_r, ang_c = grid_angles(x.shape[1])
    lbs = hgrn_lower_bounds(hgrn_lb_logits)
    for l in range(DEPTH):
        need_ctx = l < DEPTH - 1
        m_lat = [m[:, None, :] for m in jnp.split(jax.nn.silu(c) @ w_mod[l] + b_mod[l], N_MOD, axis=-1)]
        m_ctx = jnp.split(jax.nn.silu(c_ctx) @ w_mod[l] + b_mod[l], N_MOD, axis=-1)
        g = norm_g[l]
        x = half_ffn(x, m_lat[0], m_lat[1], m_lat[2], g[0], g[1], ffn_w1[l, 0], ffn_w2[l, 0])
        ctx = half_ffn(ctx, m_ctx[0], m_ctx[1], m_ctx[2], g[0], g[1], ffn_w1[l, 0], ffn_w2[l, 0])
        y_lat, y_ctx = token_mixers(modulate(rms_norm(x, g[2]), m_lat[3], m_lat[4]),
                                    modulate(rms_norm(ctx, g[2]), m_ctx[3], m_ctx[4]),
                                    need_ctx, ang_r, ang_c, lbs[l, 0], lbs[l, 1], w_in[l], hgrn_norm_g[l],
                                    qk_norm_g[l], short_conv_w[l], conf_dw_w[l], conf_dw_b[l], conf_ln_g[l],
                                    conf_ln_b[l], w_branch[l], w_out[l])
        x = x + m_lat[5] * rms_norm(y_lat, g[3])
        x = half_ffn(x, m_lat[6], m_lat[7], m_lat[8], g[4], g[5], ffn_w1[l, 1], ffn_w2[l, 1])
        if need_ctx:
            ctx = ctx + m_ctx[5] * rms_norm(y_ctx, g[3])
            ctx = half_ffn(ctx, m_ctx[6], m_ctx[7], m_ctx[8], g[4], g[5], ffn_w1[l, 1], ffn_w2[l, 1])
    return x
```

```python
import functools

import jax
import jax.numpy as jnp
from jax import lax
from jax.experimental import pallas as pl
from jax.experimental.pallas import tpu as pltpu

D_MODEL = 4096
BATCH = 8
SEQ = 2048
DEPTH = 2
GRID_W = 64
CTX_LEN = 256
N_BRANCH = 4
BRANCH_W = D_MODEL // 4
HGRN_DK = 128
HGRN_DV = 128
HGRN_HEADS = BRANCH_W // HGRN_DV
HGRN_CHUNK = 64
HEAD_DIM = 128
ATTN_Q_HEADS = BRANCH_W // HEAD_DIM
ATTN_KV_HEADS = ATTN_Q_HEADS // 4
ROPE_AXIS_DIM = HEAD_DIM // 2
ROPE_THETA = 10000.0
Q_BLOCK = 128
SHORT_CONV_W = 3
CONF_CONV_W = 31
FFN_DIM = D_MODEL
FFN_RESIDUAL = 0.5
N_MOD = 9
EPS = 1e-6
LB_FLOOR = 1e-30

HK = HGRN_HEADS * HGRN_DK
HV = HGRN_HEADS * HGRN_DV
AQ = ATTN_Q_HEADS * HEAD_DIM
AKV = ATTN_KV_HEADS * HEAD_DIM
OFF_F_FWD = 0
OFF_F_BWD = OFF_F_FWD + HK
OFF_I = OFF_F_BWD + HK
OFF_K = OFF_I + HV
OFF_V = OFF_K + AKV
CTX_SIDE_COLS = OFF_V + AKV
OFF_HQ = CTX_SIDE_COLS
OFF_HG = OFF_HQ + HK
OFF_AQ = OFF_HG + HV
OFF_SC = OFF_AQ + AQ
OFF_GLU = OFF_SC + 3 * BRANCH_W
OFF_GATE = OFF_GLU + 2 * BRANCH_W
N_IN_COLS = OFF_GATE + N_BRANCH * D_MODEL

M_LAT = BATCH * SEQ
M_CTX = BATCH * CTX_LEN
M_ALL = M_LAT + M_CTX
MOD_ROWS = 16

V7X_VMEM_BYTES = 64 * 1024 * 1024
VMEM_LIMIT = V7X_VMEM_BYTES - 8 * 1024 * 1024

BF16 = jnp.bfloat16
F32 = jnp.float32


def _params(sem):
    return pltpu.CompilerParams(dimension_semantics=sem, vmem_limit_bytes=VMEM_LIMIT)


def _mod_kernel(cc_ref, w_ref, b_ref, o_ref):
    a = cc_ref[...]
    a = (a * jax.nn.sigmoid(a)).astype(BF16)
    o_ref[...] = jnp.dot(a, w_ref[...].astype(BF16), preferred_element_type=F32) + b_ref[...]


def mod_vectors(cc, w_mod, b_mod):
    tn = 512
    n = N_MOD * D_MODEL
    return pl.pallas_call(
        _mod_kernel,
        grid=(DEPTH, n // tn),
        in_specs=[pl.BlockSpec((MOD_ROWS, D_MODEL), lambda l, j: (0, 0)),
                  pl.BlockSpec((None, D_MODEL, tn), lambda l, j: (l, 0, j)),
                  pl.BlockSpec((None, 1, tn), lambda l, j: (l, 0, j))],
        out_specs=pl.BlockSpec((None, MOD_ROWS, tn), lambda l, j: (l, 0, j)),
        out_shape=jax.ShapeDtypeStruct((DEPTH, MOD_ROWS, n), F32),
        compiler_params=_params(("arbitrary", "arbitrary")),
        name="mod_vectors",
    )(cc, w_mod, b_mod.reshape(DEPTH, 1, n))


ROW_TM = 256


def _mod_spec(chunk):
    return pl.BlockSpec((None, 1, D_MODEL), lambda i: ((i * ROW_TM) // SEQ, 0, chunk))


def _g_spec(k):
    return pl.BlockSpec((None, 1, D_MODEL), lambda i: (k, 0, 0))


def _row_spec():
    return pl.BlockSpec((ROW_TM, D_MODEL), lambda i: (i, 0))


def _rms(x):
    return x * lax.rsqrt(jnp.mean(x * x, axis=-1, keepdims=True) + EPS)


def _normmod_kernel(s_ref, g_ref, sh_ref, sc_ref, o_ref):
    y = _rms(s_ref[...]) * g_ref[...]
    o_ref[...] = (y * (1 + sc_ref[...]) + sh_ref[...]).astype(o_ref.dtype)


def normmod(s, g6, modv, g_idx, shift_idx, nrows):
    return pl.pallas_call(
        _normmod_kernel,
        grid=(nrows // ROW_TM,),
        in_specs=[_row_spec(), _g_spec(g_idx), _mod_spec(shift_idx), _mod_spec(shift_idx + 1)],
        out_specs=_row_spec(),
        out_shape=jax.ShapeDtypeStruct((nrows, D_MODEL), BF16),
        compiler_params=_params(("arbitrary",)),
        name="normmod",
    )(s, g6, modv, modv)


def _resid_kernel(s_ref, y_ref, gpost_ref, gate_ref, gpre_ref, sh_ref, sc_ref, so_ref, ho_ref, *, coef):
    yn = _rms(y_ref[...].astype(F32)) * gpost_ref[...]
    s = s_ref[...] + (coef * gate_ref[...]) * yn
    so_ref[...] = s
    h = _rms(s) * gpre_ref[...]
    ho_ref[...] = (h * (1 + sc_ref[...]) + sh_ref[...]).astype(ho_ref.dtype)


def _resid_last_kernel(s_ref, y_ref, gpost_ref, gate_ref, so_ref, *, coef):
    yn = _rms(y_ref[...].astype(F32)) * gpost_ref[...]
    so_ref[...] = s_ref[...] + (coef * gate_ref[...]) * yn


def resid_normmod(s, y, g6, modv, gpost_idx, gate_idx, coef, gpre_idx, shift_idx, nrows):
    return pl.pallas_call(
        functools.partial(_resid_kernel, coef=coef),
        grid=(nrows // ROW_TM,),
        in_specs=[_row_spec(), _row_spec(), _g_spec(gpost_idx), _mod_spec(gate_idx),
                  _g_spec(gpre_idx), _mod_spec(shift_idx), _mod_spec(shift_idx + 1)],
        out_specs=[_row_spec(), _row_spec()],
        out_shape=[jax.ShapeDtypeStruct((nrows, D_MODEL), F32), jax.ShapeDtypeStruct((nrows, D_MODEL), BF16)],
        compiler_params=_params(("arbitrary",)),
        name="resid_normmod",
    )(s, y, g6, modv, g6, modv, modv)


def resid_last(s, y, g6, modv, gpost_idx, gate_idx, coef, nrows):
    return pl.pallas_call(
        functools.partial(_resid_last_kernel, coef=coef),
        grid=(nrows // ROW_TM,),
        in_specs=[_row_spec(), _row_spec(), _g_spec(gpost_idx), _mod_spec(gate_idx)],
        out_specs=_row_spec(),
        out_shape=jax.ShapeDtypeStruct((nrows, D_MODEL), F32),
        compiler_params=_params(("arbitrary",)),
        name="resid_last",
    )(s, y, g6, modv)


MM_TM = 1024


def _mm_kernel(x_ref, w_ref, o_ref):
    o_ref[...] = jnp.dot(x_ref[...], w_ref[...], preferred_element_type=F32).astype(o_ref.dtype)


def matmul(x, w, nrows, ncols, tn, out_dtype):
    k = x.shape[1]
    return pl.pallas_call(
        _mm_kernel,
        grid=(nrows // MM_TM, ncols // tn),
        in_specs=[pl.BlockSpec((MM_TM, k), lambda i, j: (i, 0)),
                  pl.BlockSpec((k, tn), lambda i, j: (0, j))],
        out_specs=pl.BlockSpec((MM_TM, tn), lambda i, j: (i, j)),
        out_shape=jax.ShapeDtypeStruct((nrows, ncols), out_dtype),
        compiler_params=_params(("arbitrary", "arbitrary")),
        name="matmul",
    )(x, w)


def _swiglu_kernel(x_ref, wa_ref, wb_ref, o_ref):
    x = x_ref[...]
    a = jnp.dot(x, wa_ref[...], preferred_element_type=F32)
    b = jnp.dot(x, wb_ref[...], preferred_element_type=F32)
    o_ref[...] = (a * jax.nn.sigmoid(a) * b).astype(o_ref.dtype)


def swiglu_up(h, w1, nrows):
    tn = 512
    nb = FFN_DIM // tn
    return pl.pallas_call(
        _swiglu_kernel,
        grid=(nrows // MM_TM, nb),
        in_specs=[pl.BlockSpec((MM_TM, D_MODEL), lambda i, j: (i, 0)),
                  pl.BlockSpec((D_MODEL, tn), lambda i, j: (0, j)),
                  pl.BlockSpec((D_MODEL, tn), lambda i, j: (0, j + nb))],
        out_specs=pl.BlockSpec((MM_TM, tn), lambda i, j: (i, j)),
        out_shape=jax.ShapeDtypeStruct((nrows, FFN_DIM), BF16),
        compiler_params=_params(("arbitrary", "arbitrary")),
        name="swiglu_up",
    )(h, w1, w1)


def _merge_kernel(h_ref, br_ref, wg_ref, wb_ref, o_ref, acc_ref):
    b = pl.program_id(2)
    g = jnp.dot(h_ref[...], wg_ref[...], preferred_element_type=F32)
    p = jnp.dot(br_ref[...], wb_ref[...], preferred_element_type=F32)
    term = jax.nn.sigmoid(g) * p

    @pl.when(b == 0)
    def _():
        acc_ref[...] = term

    @pl.when(b > 0)
    def _():
        acc_ref[...] += term

    @pl.when(b == N_BRANCH - 1)
    def _():
        o_ref[...] = acc_ref[...].astype(o_ref.dtype)


def gated_merge(h, br, w_in, w_branch, nrows):
    tn = 512
    nb = D_MODEL // tn
    gate0 = OFF_GATE // tn
    return pl.pallas_call(
        _merge_kernel,
        grid=(nrows // MM_TM, nb, N_BRANCH),
        in_specs=[pl.BlockSpec((MM_TM, D_MODEL), lambda i, j, b: (i, 0)),
                  pl.BlockSpec((MM_TM, BRANCH_W), lambda i, j, b: (i, b)),
                  pl.BlockSpec((D_MODEL, tn), lambda i, j, b: (0, gate0 + b * nb + j)),
                  pl.BlockSpec((None, BRANCH_W, tn), lambda i, j, b: (b, 0, j))],
        out_specs=pl.BlockSpec((MM_TM, tn), lambda i, j, b: (i, j)),
        out_shape=jax.ShapeDtypeStruct((nrows, D_MODEL), BF16),
        scratch_shapes=[pltpu.VMEM((MM_TM, tn), F32)],
        compiler_params=_params(("arbitrary", "arbitrary", "arbitrary")),
        name="gated_merge",
    )(h, br, w_in, w_branch)


def _rms_norm(x, g):
    xf = x.astype(F32)
    return xf * lax.rsqrt(jnp.mean(xf * xf, axis=-1, keepdims=True) + EPS) * g.astype(F32)


def _layer_norm(x, g, b):
    xc = x - jnp.mean(x, axis=-1, keepdims=True)
    return xc * lax.rsqrt(jnp.mean(xc * xc, axis=-1, keepdims=True) + EPS) * g + b


def _dwconv(u, w):
    width = w.shape[0]
    return lax.conv_general_dilated(u, w[:, None, :], window_strides=(1,), padding=[(width // 2, width // 2)],
                                    dimension_numbers=('NWC', 'WIO', 'NWC'), feature_group_count=u.shape[-1])


def _heads(a, n, d):
    return a.reshape(a.shape[0], a.shape[1], n, d)


def _grid_angles(n_tokens):
    rows = n_tokens // GRID_W
    r, col = jnp.meshgrid(jnp.arange(rows, dtype=F32), jnp.arange(GRID_W, dtype=F32), indexing='ij')
    inv = ROPE_THETA ** (-jnp.arange(0, ROPE_AXIS_DIM, 2, dtype=F32) / ROPE_AXIS_DIM)
    return r.reshape(-1, 1) * inv, col.reshape(-1, 1) * inv


def _rope_axis(u, ang):
    u1, u2 = jnp.split(u, 2, axis=-1)
    cos = jnp.cos(ang)[None, :, None, :]
    sin = jnp.sin(ang)[None, :, None, :]
    return jnp.concatenate([u1 * cos - u2 * sin, u1 * sin + u2 * cos], axis=-1)


def _rope_2d(u, ang_r, ang_c):
    return jnp.concatenate([_rope_axis(u[..., :ROPE_AXIS_DIM], ang_r), _rope_axis(u[..., ROPE_AXIS_DIM:], ang_c)], axis=-1)


def _gqa(q, k, v):
    b, t, hq, d = q.shape
    hkv = k.shape[2]
    qg = q.reshape(b, t, hkv, hq // hkv, d)
    s = jnp.einsum('bqhgd,bkhd->bhgqk', qg, k) * (d ** -0.5)
    p = jax.nn.softmax(s, axis=-1)
    return jnp.einsum('bhgqk,bkhd->bqhgd', p, v).reshape(b, t, hq * d)


def _log_forget(a, lb):
    return jnp.logaddexp(jnp.log(jnp.maximum(lb, LB_FLOOR)), jnp.log1p(-lb) + jax.nn.log_sigmoid(a))


def _gla(q, log_f, v, s0):
    b, t, h, _ = q.shape
    dv = v.shape[-1]
    n = t // HGRN_CHUNK
    k = -jnp.expm1(log_f)

    def chunks(a):
        return a.reshape(b, n, HGRN_CHUNK, h, a.shape[-1]).transpose(1, 0, 3, 2, 4)

    lower = jnp.tril(jnp.ones((HGRN_CHUNK, HGRN_CHUNK), dtype=bool))[:, :, None]

    def step(s, blk):
        qb, kb, vb, fb = blk
        cum = jnp.cumsum(fb, axis=2)
        diff = cum[:, :, :, None, :] - cum[:, :, None, :, :]
        rel = jnp.where(lower, jnp.exp(jnp.minimum(diff, 0.0)), 0.0)
        scores = jnp.einsum('bhik,bhjk,bhijk->bhij', qb, kb, rel)
        out = jnp.einsum('bhij,bhjv->bhiv', scores, vb) + jnp.einsum('bhik,bhkv->bhiv', qb * jnp.exp(cum), s)
        last = cum[:, :, -1:, :]
        s = jnp.exp(last[:, :, 0, :, None]) * s + jnp.einsum('bhjk,bhjv->bhkv', kb * jnp.exp(last - cum), vb)
        return s, out

    s_fin, out = lax.scan(step, s0, (chunks(q), chunks(k), chunks(v), chunks(log_f)))
    return out.transpose(1, 0, 3, 2, 4).reshape(b, t, h, dv), s_fin


def _flip(a):
    return jnp.flip(a, axis=1)


def _bidir(q, lf_f, lf_b, v, s_f0, s_b0):
    o_f, s_f = _gla(q, lf_f, v, s_f0)
    o_b, s_b = _gla(_flip(q), _flip(lf_b), _flip(v), s_b0)
    return o_f + _flip(o_b), s_f, s_b


def _forget_value(z, lb_f, lb_b):
    lf_f = _heads(_log_forget(z[..., OFF_F_FWD:OFF_F_BWD], lb_f), HGRN_HEADS, HGRN_DK)
    lf_b = _heads(_log_forget(z[..., OFF_F_BWD:OFF_I], lb_b), HGRN_HEADS, HGRN_DK)
    v = _heads(z[..., OFF_I:OFF_K], HGRN_HEADS, HGRN_DV)
    return lf_f, lf_b, v


def _readout(o, g_pre, norm_g):
    y = _rms_norm(o, norm_g) * jax.nn.sigmoid(_heads(g_pre, HGRN_HEADS, HGRN_DV))
    return y.reshape(o.shape[0], o.shape[1], HV)


def _conv_branches(z, short_w, dw_w, dw_b, ln_g, ln_b):
    bg, cg, u = jnp.split(z[..., OFF_SC:OFF_GLU], 3, axis=-1)
    sc = bg * _dwconv(cg * u, short_w)
    a, gt = jnp.split(z[..., OFF_GLU:OFF_GATE], 2, axis=-1)
    cf = jax.nn.silu(_layer_norm(_dwconv(a * jax.nn.sigmoid(gt), dw_w) + dw_b, ln_g, ln_b))
    return sc, cf


def _mixers(z_lat, z_ctx, need_ctx, ang_r, ang_c, lb_f, lb_b, hgrn_norm_g, qk_norm_g,
            short_w, dw_w, dw_b, ln_g, ln_b):
    lf_cf, lf_cb, v_c = _forget_value(z_ctx, lb_f, lb_b)
    lf_lf, lf_lb, v_l = _forget_value(z_lat, lb_f, lb_b)
    zero = jnp.zeros((BATCH, HGRN_HEADS, HGRN_DK, HGRN_DV), F32)
    q_c = _heads(z_ctx[..., OFF_HQ:OFF_HG], HGRN_HEADS, HGRN_DK)
    o_c, s_f, s_b = _bidir(q_c, lf_cf, lf_cb, v_c, zero, zero)
    o_l, _, _ = _bidir(_heads(z_lat[..., OFF_HQ:OFF_HG], HGRN_HEADS, HGRN_DK), lf_lf, lf_lb, v_l, s_f, s_b)
    hgrn_lat = _readout(o_l, z_lat[..., OFF_HG:OFF_AQ], hgrn_norm_g)

    k_c = _rms_norm(_heads(z_ctx[..., OFF_K:OFF_V], ATTN_KV_HEADS, HEAD_DIM), qk_norm_g[1])
    va_c = _heads(z_ctx[..., OFF_V:CTX_SIDE_COLS], ATTN_KV_HEADS, HEAD_DIM)
    q_l = _rope_2d(_rms_norm(_heads(z_lat[..., OFF_AQ:OFF_SC], ATTN_Q_HEADS, HEAD_DIM), qk_norm_g[0]), ang_r, ang_c)
    k_l = _rope_2d(_rms_norm(_heads(z_lat[..., OFF_K:OFF_V], ATTN_KV_HEADS, HEAD_DIM), qk_norm_g[1]), ang_r, ang_c)
    va_l = _heads(z_lat[..., OFF_V:CTX_SIDE_COLS], ATTN_KV_HEADS, HEAD_DIM)
    att_lat = _gqa(q_l, jnp.concatenate([k_c, k_l], axis=1), jnp.concatenate([va_c, va_l], axis=1))

    sc_lat, cf_lat = _conv_branches(z_lat, short_w, dw_w, dw_b, ln_g, ln_b)
    br_lat = jnp.concatenate([hgrn_lat, att_lat, sc_lat, cf_lat], axis=-1).astype(BF16)
    if not need_ctx:
        return br_lat, None
    hgrn_ctx = _readout(o_c, z_ctx[..., OFF_HG:OFF_AQ], hgrn_norm_g)
    qa_c = _rms_norm(_heads(z_ctx[..., OFF_AQ:OFF_SC], ATTN_Q_HEADS, HEAD_DIM), qk_norm_g[0])
    att_ctx = _gqa(qa_c, k_c, va_c)
    sc_ctx, cf_ctx = _conv_branches(z_ctx, short_w, dw_w, dw_b, ln_g, ln_b)
    br_ctx = jnp.concatenate([hgrn_ctx, att_ctx, sc_ctx, cf_ctx], axis=-1).astype(BF16)
    return br_lat, br_ctx


def kernel(x, c, ctx, c_ctx, w_mod, b_mod, norm_g, ffn_w1, ffn_w2, w_in, hgrn_lb_logits, hgrn_norm_g, qk_norm_g,
           short_conv_w, conf_dw_w, conf_dw_b, conf_ln_g, conf_ln_b, w_branch, w_out):
    ang_r, ang_c = _grid_angles(SEQ)
    p = jax.nn.softmax(hgrn_lb_logits.astype(F32), axis=0)
    lbs = jnp.cumsum(p, axis=0) - p[:1]

    cc = jnp.concatenate([c, c_ctx[None], jnp.zeros((MOD_ROWS - BATCH - 1, D_MODEL), F32)], axis=0)
    modv_all = mod_vectors(cc, w_mod, b_mod).reshape(DEPTH, MOD_ROWS, 1, N_MOD * D_MODEL)

    s = jnp.concatenate([x.reshape(M_LAT, D_MODEL), ctx.reshape(M_CTX, D_MODEL)], axis=0)
    w1_bf = ffn_w1.astype(BF16)
    w2_bf = ffn_w2.astype(BF16)
    w_in_bf = w_in.astype(BF16)
    w_branch_bf = w_branch.astype(BF16)
    w_out_bf = w_out.astype(BF16)

    for l in range(DEPTH):
        need_ctx = l < DEPTH - 1
        modv = modv_all[l]
        g6 = norm_g[l].reshape(6, 1, D_MODEL)
        m_mix = M_ALL if need_ctx else M_LAT

        h = normmod(s, g6, modv, 0, 0, M_ALL)
        y = matmul(swiglu_up(h, w1_bf[l, 0], M_ALL), w2_bf[l, 0], M_ALL, D_MODEL, 1024, BF16)
        s, h = resid_normmod(s, y, g6, modv, 1, 2, FFN_RESIDUAL, 2, 3, M_ALL)

        z = matmul(h, w_in_bf[l], M_ALL, OFF_GATE, 512, F32)
        z_lat = z[:M_LAT].reshape(BATCH, SEQ, OFF_GATE)
        z_ctx = z[M_LAT:].reshape(BATCH, CTX_LEN, OFF_GATE)
        br_lat, br_ctx = _mixers(z_lat, z_ctx, need_ctx, ang_r, ang_c, lbs[l, 0], lbs[l, 1], hgrn_norm_g[l],
                                 qk_norm_g[l], short_conv_w[l], conf_dw_w[l], conf_dw_b[l], conf_ln_g[l],
                                 conf_ln_b[l])
        br = br_lat.reshape(M_LAT, D_MODEL)
        if need_ctx:
            br = jnp.concatenate([br, br_ctx.reshape(M_CTX, D_MODEL)], axis=0)
        acc = gated_merge(h, br, w_in_bf[l], w_branch_bf[l], m_mix)
        y = matmul(acc, w_out_bf[l], m_mix, D_MODEL, 1024, BF16)
        s, h = resid_normmod(s, y, g6, modv, 3, 5, 1.0, 4, 6, m_mix)

        y = matmul(swiglu_up(h, w1_bf[l, 1], m_mix), w2_bf[l, 1], m_mix, D_MODEL, 1024, BF16)
        s = resid_last(s, y, g6, modv, 5, 8, FFN_RESIDUAL, m_mix)
    return s.reshape(BATCH, SEQ, D_MODEL)
```

```python
import functools
from typing import NamedTuple

import numpy as np
import jax
import jax.numpy as jnp
from jax import lax
from jax.experimental import pallas as pl
from jax.experimental.pallas import tpu as pltpu

D_MODEL = 4096
DEPTH = 2
GRID_W = 64
N_BRANCH = 4
BRANCH_W = D_MODEL // 4
HGRN_DK = 128
HGRN_DV = 128
HGRN_HEADS = BRANCH_W // HGRN_DV
HGRN_CHUNK = 64
HEAD_DIM = 128
ATTN_Q_HEADS = BRANCH_W // HEAD_DIM
ATTN_KV_HEADS = ATTN_Q_HEADS // 4
ATTN_GROUP = ATTN_Q_HEADS // ATTN_KV_HEADS
ROPE_AXIS_DIM = HEAD_DIM // 2
ROPE_THETA = 10000.0
SHORT_CONV_W = 3
CONF_CONV_W = 31
FFN_DIM = D_MODEL
FFN_RESIDUAL = 0.5
N_MOD = 9
EPS = 1e-6
LB_FLOOR = 1e-30

HK = HGRN_HEADS * HGRN_DK
HV = HGRN_HEADS * HGRN_DV
AQ = ATTN_Q_HEADS * HEAD_DIM
AKV = ATTN_KV_HEADS * HEAD_DIM
OFF_K = 2 * HK + HV
OFF_HQ = OFF_K + 2 * AKV
OFF_GATE = OFF_HQ + HK + HV + AQ + 3 * BRANCH_W + 2 * BRANCH_W
N_IN_COLS = OFF_GATE + N_BRANCH * D_MODEL

ZF, ZB, ZI, ZHQ, ZHG, ZAQ, ZSB, ZSC, ZSU, ZGA, ZGG = (k * BRANCH_W for k in range(11))
ZK = 11 * BRANCH_W
ZV = ZK + AKV
Z_COLS = ZV + AKV
assert Z_COLS == OFF_GATE

MOD_ROWS = 16
HALO = 16
LANES = 128

V7X_VMEM_BYTES = 64 * 1024 * 1024
VMEM_LIMIT = V7X_VMEM_BYTES - 8 * 1024 * 1024

BF16 = jnp.bfloat16
F32 = jnp.float32


class Dims(NamedTuple):
    nb: int
    t_lat: int
    t_ctx: int

    @property
    def pb(self):
        return self.t_lat + self.t_ctx

    @property
    def m_all(self):
        return self.nb * self.pb

    @property
    def tiles_per_batch(self):
        return self.pb // self.t_ctx

    @property
    def mm_tm(self):
        return next(t for t in (1024, 512, 256, 128) if self.m_all % t == 0)


def _params(sem):
    return pltpu.CompilerParams(dimension_semantics=sem, vmem_limit_bytes=VMEM_LIMIT)


def _mod_kernel(cc_ref, w_ref, b_ref, o_ref):
    a = cc_ref[...]
    a = (a * jax.nn.sigmoid(a)).astype(BF16)
    o_ref[...] = jnp.dot(a, w_ref[...].astype(BF16), preferred_element_type=F32) + b_ref[...]


def mod_vectors(cc, w_mod, b_mod):
    tn = 512
    n = N_MOD * D_MODEL
    return pl.pallas_call(
        _mod_kernel,
        grid=(DEPTH, n // tn),
        in_specs=[pl.BlockSpec((MOD_ROWS, D_MODEL), lambda l, j: (0, 0)),
                  pl.BlockSpec((None, D_MODEL, tn), lambda l, j: (l, 0, j)),
                  pl.BlockSpec((None, 1, tn), lambda l, j: (l, 0, j))],
        out_specs=pl.BlockSpec((None, MOD_ROWS, tn), lambda l, j: (l, 0, j)),
        out_shape=jax.ShapeDtypeStruct((DEPTH, MOD_ROWS, n), F32),
        compiler_params=_params(("arbitrary", "arbitrary")),
        name="mod_vectors",
    )(cc, w_mod, b_mod.reshape(DEPTH, 1, n))


def _mod_spec(d, chunk):
    tpb = d.tiles_per_batch
    return pl.BlockSpec((None, 1, D_MODEL), lambda i: (jnp.where(i % tpb == 0, d.nb, i // tpb), 0, chunk))


def _g_spec(k):
    return pl.BlockSpec((None, 1, D_MODEL), lambda i: (k, 0, 0))


def _row_spec(d):
    return pl.BlockSpec((d.t_ctx, D_MODEL), lambda i: (i, 0))


def _lat_in_spec(d, chunk=None):
    tpb = d.tiles_per_batch
    if chunk is None:
        return pl.BlockSpec((d.t_ctx, D_MODEL), lambda b, j: (b * tpb + 1 + j, 0))
    return pl.BlockSpec((None, 1, D_MODEL), lambda b, j: (b, 0, chunk))


def _rms(x):
    return x * lax.rsqrt(jnp.mean(x * x, axis=-1, keepdims=True) + EPS)


def _normmod_kernel(s_ref, g_ref, sh_ref, sc_ref, o_ref):
    y = _rms(s_ref[...]) * g_ref[...]
    o_ref[...] = (y * (1 + sc_ref[...]) + sh_ref[...]).astype(o_ref.dtype)


def normmod(d, s, g6, modv, g_idx, shift_idx):
    return pl.pallas_call(
        _normmod_kernel,
        grid=(d.m_all // d.t_ctx,),
        in_specs=[_row_spec(d), _g_spec(g_idx), _mod_spec(d, shift_idx), _mod_spec(d, shift_idx + 1)],
        out_specs=_row_spec(d),
        out_shape=jax.ShapeDtypeStruct((d.m_all, D_MODEL), BF16),
        compiler_params=_params(("arbitrary",)),
        name="normmod",
    )(s, g6, modv, modv)


def _resid_kernel(s_ref, y_ref, gpost_ref, gate_ref, gpre_ref, sh_ref, sc_ref, so_ref, ho_ref, *, coef):
    yn = _rms(y_ref[...].astype(F32)) * gpost_ref[...]
    s = s_ref[...] + (coef * gate_ref[...]) * yn
    so_ref[...] = s
    h = _rms(s) * gpre_ref[...]
    ho_ref[...] = (h * (1 + sc_ref[...]) + sh_ref[...]).astype(ho_ref.dtype)


def _resid_last_kernel(s_ref, y_ref, gpost_ref, gate_ref, so_ref, *, coef):
    yn = _rms(y_ref[...].astype(F32)) * gpost_ref[...]
    so_ref[...] = s_ref[...] + (coef * gate_ref[...]) * yn


def resid_normmod(d, s, y, g6, modv, gpost_idx, gate_idx, coef, gpre_idx, shift_idx):
    return pl.pallas_call(
        functools.partial(_resid_kernel, coef=coef),
        grid=(d.m_all // d.t_ctx,),
        in_specs=[_row_spec(d), _row_spec(d), _g_spec(gpost_idx), _mod_spec(d, gate_idx),
                  _g_spec(gpre_idx), _mod_spec(d, shift_idx), _mod_spec(d, shift_idx + 1)],
        out_specs=[_row_spec(d), _row_spec(d)],
        out_shape=[jax.ShapeDtypeStruct((d.m_all, D_MODEL), F32), jax.ShapeDtypeStruct((d.m_all, D_MODEL), BF16)],
        compiler_params=_params(("arbitrary",)),
        name="resid_normmod",
    )(s, y, g6, modv, g6, modv, modv)


def resid_normmod_latent(d, s, y, g6, modv, gpost_idx, gate_idx, coef, gpre_idx, shift_idx):
    nl = d.t_lat // d.t_ctx
    g_spec = lambda k: pl.BlockSpec((None, 1, D_MODEL), lambda b, j: (k, 0, 0))
    dense = pl.BlockSpec((d.t_ctx, D_MODEL), lambda b, j: (b * nl + j, 0))
    return pl.pallas_call(
        functools.partial(_resid_kernel, coef=coef),
        grid=(d.nb, nl),
        in_specs=[_lat_in_spec(d), _lat_in_spec(d), g_spec(gpost_idx), _lat_in_spec(d, gate_idx),
                  g_spec(gpre_idx), _lat_in_spec(d, shift_idx), _lat_in_spec(d, shift_idx + 1)],
        out_specs=[dense, dense],
        out_shape=[jax.ShapeDtypeStruct((d.nb * d.t_lat, D_MODEL), F32),
                   jax.ShapeDtypeStruct((d.nb * d.t_lat, D_MODEL), BF16)],
        compiler_params=_params(("arbitrary", "arbitrary")),
        name="resid_normmod_latent",
    )(s, y, g6, modv, g6, modv, modv)


def resid_last(d, s, y, g6, modv, gpost_idx, gate_idx, coef, dense_latent):
    if dense_latent:
        nl = d.t_lat // d.t_ctx
        dense = pl.BlockSpec((d.t_ctx, D_MODEL), lambda b, j: (b * nl + j, 0))
        return pl.pallas_call(
            functools.partial(_resid_last_kernel, coef=coef),
            grid=(d.nb, nl),
            in_specs=[dense, dense, pl.BlockSpec((None, 1, D_MODEL), lambda b, j: (gpost_idx, 0, 0)),
                      pl.BlockSpec((None, 1, D_MODEL), lambda b, j: (b, 0, gate_idx))],
            out_specs=dense,
            out_shape=jax.ShapeDtypeStruct((d.nb * d.t_lat, D_MODEL), F32),
            compiler_params=_params(("arbitrary", "arbitrary")),
            name="resid_last_latent",
        )(s, y, g6, modv)
    return pl.pallas_call(
        functools.partial(_resid_last_kernel, coef=coef),
        grid=(d.m_all // d.t_ctx,),
        in_specs=[_row_spec(d), _row_spec(d), _g_spec(gpost_idx), _mod_spec(d, gate_idx)],
        out_specs=_row_spec(d),
        out_shape=jax.ShapeDtypeStruct((d.m_all, D_MODEL), F32),
        compiler_params=_params(("arbitrary",)),
        name="resid_last",
    )(s, y, g6, modv)


def _row_tile(m):
    return next(t for t in (1024, 512, 256, 128) if m % t == 0)


def _mm_kernel(x_ref, w_ref, o_ref):
    o_ref[...] = jnp.dot(x_ref[...], w_ref[...], preferred_element_type=F32).astype(o_ref.dtype)


def matmul(x, w, ncols, tn, out_dtype):
    m, k = x.shape
    tm = _row_tile(m)
    return pl.pallas_call(
        _mm_kernel,
        grid=(m // tm, ncols // tn),
        in_specs=[pl.BlockSpec((tm, k), lambda i, j: (i, 0)),
                  pl.BlockSpec((k, tn), lambda i, j: (0, j))],
        out_specs=pl.BlockSpec((tm, tn), lambda i, j: (i, j)),
        out_shape=jax.ShapeDtypeStruct((m, ncols), out_dtype),
        compiler_params=_params(("arbitrary", "arbitrary")),
        name="matmul",
    )(x, w)


def _swiglu_kernel(x_ref, wa_ref, wb_ref, o_ref):
    x = x_ref[...]
    a = jnp.dot(x, wa_ref[...], preferred_element_type=F32)
    b = jnp.dot(x, wb_ref[...], preferred_element_type=F32)
    o_ref[...] = (a * jax.nn.sigmoid(a) * b).astype(o_ref.dtype)


def swiglu_up(h, w1):
    m = h.shape[0]
    tm = _row_tile(m)
    tn = 512
    nb = FFN_DIM // tn
    return pl.pallas_call(
        _swiglu_kernel,
        grid=(m // tm, nb),
        in_specs=[pl.BlockSpec((tm, D_MODEL), lambda i, j: (i, 0)),
                  pl.BlockSpec((D_MODEL, tn), lambda i, j: (0, j)),
                  pl.BlockSpec((D_MODEL, tn), lambda i, j: (0, j + nb))],
        out_specs=pl.BlockSpec((tm, tn), lambda i, j: (i, j)),
        out_shape=jax.ShapeDtypeStruct((m, FFN_DIM), BF16),
        compiler_params=_params(("arbitrary", "arbitrary")),
        name="swiglu_up",
    )(h, w1, w1)


def _merge_kernel(h_ref, br_ref, wg_ref, wb_ref, o_ref, acc_ref):
    b = pl.program_id(2)
    g = jnp.dot(h_ref[...], wg_ref[...], preferred_element_type=F32)
    p = jnp.dot(br_ref[...], wb_ref[...], preferred_element_type=F32)
    term = jax.nn.sigmoid(g) * p

    @pl.when(b == 0)
    def _():
        acc_ref[...] = term

    @pl.when(b > 0)
    def _():
        acc_ref[...] += term

    @pl.when(b == N_BRANCH - 1)
    def _():
        o_ref[...] = acc_ref[...].astype(o_ref.dtype)


def gated_merge(h, br, w_in, w_branch):
    m = h.shape[0]
    tm = _row_tile(m)
    tn = 512
    nb = D_MODEL // tn
    gate0 = OFF_GATE // tn
    return pl.pallas_call(
        _merge_kernel,
        grid=(m // tm, nb, N_BRANCH),
        in_specs=[pl.BlockSpec((tm, D_MODEL), lambda i, j, b: (i, 0)),
                  pl.BlockSpec((tm, BRANCH_W), lambda i, j, b: (i, b)),
                  pl.BlockSpec((D_MODEL, tn), lambda i, j, b: (0, gate0 + b * nb + j)),
                  pl.BlockSpec((None, BRANCH_W, tn), lambda i, j, b: (b, 0, j))],
        out_specs=pl.BlockSpec((tm, tn), lambda i, j, b: (i, j)),
        out_shape=jax.ShapeDtypeStruct((m, D_MODEL), BF16),
        scratch_shapes=[pltpu.VMEM((tm, tn), F32)],
        compiler_params=_params(("arbitrary", "arbitrary", "arbitrary")),
        name="gated_merge",
    )(h, br, w_in, w_branch)


_GLA_LEVELS = (8, 16, 32)
_GLA_G_ROWS = (2 + 2 * len(_GLA_LEVELS)) * HGRN_CHUNK


def _gla_constants():
    c = HGRN_CHUNK
    i = np.arange(c)[:, None]
    t = np.arange(c)[None, :]
    mats = [t <= i]
    masks = []
    for s in _GLA_LEVELS:
        blk = i // s
        right = blk % 2 == 1
        mats.append(right & (t > blk * s) & (t <= i))
        mats.append(~right & (t > i) & (t <= (blk + 1) * s))
        masks.append(right & (t // s == blk - 1))
    mats.append(t > i)
    g_f = np.concatenate(mats, 0).astype(np.float32)
    g_b = np.concatenate([m[::-1, ::-1] for m in mats], 0).astype(np.float32)
    m_f = np.stack(masks).astype(np.float32)
    m_b = np.stack([m[::-1, ::-1] for m in masks]).astype(np.float32)
    tile3 = lambda g: jnp.asarray(np.tile(g, (1, 3)), BF16)
    return tile3(g_f), tile3(g_b), jnp.asarray(m_f), jnp.asarray(m_b)


def _nt_dot(a, b):
    return lax.dot_general(a, b, (((1,), (1,)), ((), ())), preferred_element_type=F32)


def _gla_chunk(a, q, v, lb_floor, one_minus_lb, gmat, masks, st, reverse):
    c = HGRN_CHUNK
    nsub = c // 8
    f = lb_floor + one_minus_lb * jax.nn.sigmoid(a)
    lf = jnp.log(f)
    k = 1.0 - f
    hi = lf.astype(BF16)
    r1 = lf - hi.astype(F32)
    mid = r1.astype(BF16)
    lo = (r1 - mid.astype(F32)).astype(BF16)
    ex = jnp.dot(gmat, jnp.concatenate([hi, mid, lo], axis=0), preferred_element_type=F32)
    cum = ex[0:c]
    vb16 = v.astype(BF16)

    p = jnp.zeros((c, c), F32)
    for li in range(len(_GLA_LEVELS)):
        ea = ex[c * (1 + 2 * li):c * (2 + 2 * li)]
        eb = ex[c * (2 + 2 * li):c * (3 + 2 * li)]
        qa = (q * jnp.exp(jnp.minimum(ea, 0.0))).astype(BF16)
        kb = (k * jnp.exp(jnp.minimum(eb, 0.0))).astype(BF16)
        p = p + masks[li] * _nt_dot(qa, kb)
    o = jnp.dot(p.astype(BF16), vb16, preferred_element_type=F32)

    q3 = q.reshape(nsub, 8, LANES)
    k3 = k.reshape(nsub, 8, LANES)
    c3 = cum.reshape(nsub, 8, LANES)
    v3 = v.reshape(nsub, 8, LANES)
    ii = lax.broadcasted_iota(jnp.int32, (nsub, 8, LANES), 1)
    ws = []
    for jj in range(8):
        kj = jnp.broadcast_to(k3[:, jj:jj + 1, :], (nsub, 8, LANES))
        cj = jnp.broadcast_to(c3[:, jj:jj + 1, :], (nsub, 8, LANES))
        w = q3 * kj * jnp.exp(jnp.minimum(c3 - cj, 0.0))
        keep = (ii <= jj) if reverse else (ii >= jj)
        ws.append(jnp.where(keep, w, 0.0).reshape(c, LANES).astype(BF16))
    rs = jnp.dot(jnp.concatenate(ws, axis=0), jnp.ones((LANES, LANES), BF16), preferred_element_type=F32)
    o3 = o.reshape(nsub, 8, LANES)
    for jj in range(8):
        vj = jnp.broadcast_to(v3[:, jj:jj + 1, :], (nsub, 8, LANES))
        o3 = o3 + rs[jj * c:(jj + 1) * c].reshape(nsub, 8, LANES) * vj
    o = o3.reshape(c, LANES)

    o = o + _nt_dot((q * jnp.exp(cum)).astype(BF16), st.astype(BF16))
    ke = (k * jnp.exp(ex[_GLA_G_ROWS - c:_GLA_G_ROWS])).astype(BF16)
    total = cum[0:1] if reverse else cum[c - 1:c]
    st_new = jnp.exp(total) * st + jnp.dot(v.T.astype(BF16), ke, preferred_element_type=F32)
    return o, st_new


def _gla_kernel(af_ref, ab_ref, v_ref, q_ref, g_ref, lbl_ref, ng_ref, gf_ref, gb_ref, mf_ref, mb_ref,
                o_ref, of_ref, ob_ref, sf_ref, sb_ref, *, layer, t_ctx, t_lat):
    c = HGRN_CHUNK
    logits = lbl_ref[...]
    e = jnp.exp(logits - jnp.max(logits, axis=0, keepdims=True))
    pr = e / jnp.sum(e, axis=0, keepdims=True)
    csum = pr[0]
    for dd in range(1, layer + 1):
        csum = csum + pr[dd]
    lb = csum - pr[0]
    lb_floor = jnp.maximum(lb, LB_FLOOR)
    one_minus_lb = 1.0 - lb
    masks_f = [mf_ref[li] for li in range(len(_GLA_LEVELS))]
    masks_b = [mb_ref[li] for li in range(len(_GLA_LEVELS))]
    gmat_f = gf_ref[...]
    gmat_b = gb_ref[...]

    sf_ref[...] = jnp.zeros_like(sf_ref)
    sb_ref[...] = jnp.zeros_like(sb_ref)

    def scan(row0, nchunks):
        def body(ci, carry):
            rf = pl.multiple_of(row0 + ci * c, c)
            rb = pl.multiple_of(row0 + (nchunks - 1 - ci) * c, c)
            o_f, st_f = _gla_chunk(af_ref[pl.ds(rf, c), :], q_ref[pl.ds(rf, c), :], v_ref[pl.ds(rf, c), :],
                                   lb_floor[0:1], one_minus_lb[0:1], gmat_f, masks_f, sf_ref[...], False)
            o_b, st_b = _gla_chunk(ab_ref[pl.ds(rb, c), :], q_ref[pl.ds(rb, c), :], v_ref[pl.ds(rb, c), :],
                                   lb_floor[1:2], one_minus_lb[1:2], gmat_b, masks_b, sb_ref[...], True)
            of_ref[pl.ds(rf, c), :] = o_f
            ob_ref[pl.ds(rb, c), :] = o_b
            sf_ref[...] = st_f
            sb_ref[...] = st_b
            return carry
        lax.fori_loop(0, nchunks, body, 0)

    scan(0, t_ctx // c)
    scan(t_ctx, t_lat // c)

    o = of_ref[...] + ob_ref[...]
    y = _rms(o) * ng_ref[...] * jax.nn.sigmoid(g_ref[...])
    o_ref[...] = y.astype(o_ref.dtype)


def hgrn_branch(d, z, lb_logits, norm_g, layer, consts):
    g_f, g_b, m_f, m_b = consts
    col = lambda base: (lambda b, h: (b, base // LANES + h))
    zspec = lambda base: pl.BlockSpec((d.pb, LANES), col(base))
    const2 = lambda shape: pl.BlockSpec(shape, lambda b, h: (0,) * len(shape))
    return pl.pallas_call(
        functools.partial(_gla_kernel, layer=layer, t_ctx=d.t_ctx, t_lat=d.t_lat),
        grid=(d.nb, HGRN_HEADS),
        in_specs=[zspec(ZF), zspec(ZB), zspec(ZI), zspec(ZHQ), zspec(ZHG),
                  pl.BlockSpec((DEPTH, 2, LANES), lambda b, h: (0, 0, h)),
                  const2((1, HGRN_DV)), const2(g_f.shape), const2(g_b.shape), const2(m_f.shape), const2(m_b.shape)],
        out_specs=pl.BlockSpec((d.pb, LANES), lambda b, h: (b, h)),
        out_shape=jax.ShapeDtypeStruct((d.m_all, D_MODEL), BF16),
        scratch_shapes=[pltpu.VMEM((d.pb, LANES), F32), pltpu.VMEM((d.pb, LANES), F32),
                        pltpu.VMEM((HGRN_DV, HGRN_DK), F32), pltpu.VMEM((HGRN_DV, HGRN_DK), F32)],
        compiler_params=_params(("arbitrary", "arbitrary")),
        name="hgrn_branch",
    )(z, z, z, z, z, lb_logits, norm_g.reshape(1, HGRN_DV), g_f, g_b, m_f, m_b)


def _rope_tables(d):
    half = ROPE_AXIS_DIM // 2
    tok = np.arange(d.t_lat)
    inv = ROPE_THETA ** (-np.arange(0, ROPE_AXIS_DIM, 2, dtype=np.float32) / ROPE_AXIS_DIM)
    ang_r = (tok // GRID_W).astype(np.float32)[:, None] * inv.astype(np.float32)
    ang_c = (tok % GRID_W).astype(np.float32)[:, None] * inv.astype(np.float32)
    ang = jnp.asarray(np.concatenate([ang_r, ang_r, ang_c, ang_c], axis=1), F32)
    first = jnp.asarray((np.arange(HEAD_DIM) % ROPE_AXIS_DIM) < half)
    cos = jnp.cos(ang)
    sin = jnp.sin(ang)
    sin_a = jnp.where(first, -sin, 0.0)
    sin_b = jnp.where(first, 0.0, sin)
    pad = lambda t, fill: jnp.concatenate([jnp.full((d.t_ctx, HEAD_DIM), fill, F32), t], axis=0)
    return pad(cos, 1.0), pad(sin_a, 0.0), pad(sin_b, 0.0)


def _rope(u, cos, sin_a, sin_b):
    half = ROPE_AXIS_DIM // 2
    return u * cos + pltpu.roll(u, HEAD_DIM - half, 1) * sin_a + pltpu.roll(u, half, 1) * sin_b


def _attn_kernel(q_ref, k_ref, v_ref, g_ref, ck_ref, sak_ref, sbk_ref, cq_ref, saq_ref, sbq_ref, br_ref,
                 o_ref, kbuf, vbuf, *, t_ctx):
    del br_ref
    qi = pl.program_id(2)
    tq = q_ref.shape[0]
    scale = HEAD_DIM ** -0.5

    @pl.when(qi == 0)
    def _():
        kn = _rms(k_ref[...]) * g_ref[1:2, :]
        kbuf[...] = _rope(kn, ck_ref[...], sak_ref[...], sbk_ref[...]).astype(BF16)
        vbuf[...] = v_ref[...].astype(BF16)

    q = q_ref[...]
    cq, saq, sbq = cq_ref[...], saq_ref[...], sbq_ref[...]
    qs = []
    for gi in range(ATTN_GROUP):
        qn = _rms(q[:, gi * HEAD_DIM:(gi + 1) * HEAD_DIM]) * g_ref[0:1, :]
        qs.append(_rope(qn, cq, saq, sbq).astype(BF16))
    q4 = jnp.concatenate(qs, axis=0)

    def attend(keys, vals):
        s = _nt_dot(q4, keys)
        m = jnp.max(s, axis=-1, keepdims=True)
        p = jnp.exp((s - m) * scale)
        l = jnp.sum(p, axis=-1, keepdims=True)
        o = jnp.dot(p.astype(BF16), vals, preferred_element_type=F32) / l
        for gi in range(ATTN_GROUP):
            o_ref[:, gi * HEAD_DIM:(gi + 1) * HEAD_DIM] = o[gi * tq:(gi + 1) * tq].astype(o_ref.dtype)

    @pl.when(qi == 0)
    def _():
        attend(kbuf[0:t_ctx, :], vbuf[0:t_ctx, :])

    @pl.when(qi > 0)
    def _():
        attend(kbuf[...], vbuf[...])


def attention_branch(d, z, br, qk_norm_g, tables):
    cos, sin_a, sin_b = tables
    tq = d.t_ctx
    nq = d.tiles_per_batch
    gw = ATTN_GROUP * HEAD_DIM
    kspec = lambda base: pl.BlockSpec((d.pb, HEAD_DIM), lambda b, h, qi: (b, base // HEAD_DIM + h))
    tab_k = pl.BlockSpec((d.pb, HEAD_DIM), lambda b, h, qi: (0, 0))
    tab_q = pl.BlockSpec((tq, HEAD_DIM), lambda b, h, qi: (qi, 0))
    return pl.pallas_call(
        functools.partial(_attn_kernel, t_ctx=d.t_ctx),
        grid=(d.nb, ATTN_KV_HEADS, nq),
        in_specs=[pl.BlockSpec((tq, gw), lambda b, h, qi: (b * nq + qi, ZAQ // gw + h)),
                  kspec(ZK), kspec(ZV),
                  pl.BlockSpec((2, HEAD_DIM), lambda b, h, qi: (0, 0)),
                  tab_k, tab_k, tab_k, tab_q, tab_q, tab_q,
                  pl.BlockSpec(memory_space=pl.ANY)],
        out_specs=pl.BlockSpec((tq, gw), lambda b, h, qi: (b * nq + qi, BRANCH_W // gw + h)),
        out_shape=jax.ShapeDtypeStruct((d.m_all, D_MODEL), BF16),
        scratch_shapes=[pltpu.VMEM((d.pb, HEAD_DIM), BF16), pltpu.VMEM((d.pb, HEAD_DIM), BF16)],
        input_output_aliases={10: 0},
        compiler_params=_params(("arbitrary", "arbitrary", "arbitrary")),
        name="attention_branch",
    )(z, z, z, qk_norm_g, cos, sin_a, sin_b, cos, sin_a, sin_b, br)


def _conv_kernel(sb_ref, sc_ref, su_ref, ga_ref, gg_ref,
                 scp_ref, sup_ref, gap_ref, ggp_ref, scn_ref, sun_ref, gan_ref, ggn_ref,
                 ws_ref, wc_ref, bc_ref, lg_ref, lb_ref, br_ref, o_ref, ps_ref, pc_ref, *, tiles_per_batch):
    del br_ref
    tt = sb_ref.shape[0]
    within = pl.program_id(0) % tiles_per_batch
    prev_ok = (within >= 2).astype(F32)
    next_ok = jnp.logical_and(within >= 1, within <= tiles_per_batch - 2).astype(F32)

    glu = lambda a, g: a * jax.nn.sigmoid(g)
    ps_ref[0:HALO, :] = scp_ref[...] * sup_ref[...] * prev_ok
    ps_ref[HALO:HALO + tt, :] = sc_ref[...] * su_ref[...]
    ps_ref[HALO + tt:2 * HALO + tt, :] = scn_ref[...] * sun_ref[...] * next_ok
    pc_ref[0:HALO, :] = glu(gap_ref[...], ggp_ref[...]) * prev_ok
    pc_ref[HALO:HALO + tt, :] = glu(ga_ref[...], gg_ref[...])
    pc_ref[HALO + tt:2 * HALO + tt, :] = glu(gan_ref[...], ggn_ref[...]) * next_ok

    acc = jnp.zeros((tt, BRANCH_W), F32)
    for tau in range(SHORT_CONV_W):
        acc = acc + ps_ref[pl.ds(HALO - SHORT_CONV_W // 2 + tau, tt), :] * ws_ref[tau:tau + 1, :]
    o_ref[:, 0:BRANCH_W] = (sb_ref[...] * acc).astype(o_ref.dtype)

    acc = jnp.zeros((tt, BRANCH_W), F32)
    for tau in range(CONF_CONV_W):
        acc = acc + pc_ref[pl.ds(HALO - CONF_CONV_W // 2 + tau, tt), :] * wc_ref[tau:tau + 1, :]
    u = acc + bc_ref[...]
    uc = u - jnp.mean(u, axis=-1, keepdims=True)
    y = uc * lax.rsqrt(jnp.mean(uc * uc, axis=-1, keepdims=True) + EPS) * lg_ref[...] + lb_ref[...]
    o_ref[:, BRANCH_W:2 * BRANCH_W] = (y * jax.nn.sigmoid(y)).astype(o_ref.dtype)


def conv_branches(d, z, br, short_w, dw_w, dw_b, ln_g, ln_b):
    tt = d.t_ctx
    per = tt // HALO
    last = d.m_all // HALO - 1
    cur = lambda base: pl.BlockSpec((tt, BRANCH_W), lambda i: (i, base // BRANCH_W))
    prv = lambda base: pl.BlockSpec((HALO, BRANCH_W), lambda i: (jnp.maximum(i * per - 1, 0), base // BRANCH_W))
    nxt = lambda base: pl.BlockSpec((HALO, BRANCH_W), lambda i: (jnp.minimum((i + 1) * per, last), base // BRANCH_W))
    full = lambda shape: pl.BlockSpec(shape, lambda i: (0,) * len(shape))
    row = full((1, BRANCH_W))
    return pl.pallas_call(
        functools.partial(_conv_kernel, tiles_per_batch=d.tiles_per_batch),
        grid=(d.m_all // tt,),
        in_specs=[cur(ZSB), cur(ZSC), cur(ZSU), cur(ZGA), cur(ZGG),
                  prv(ZSC), prv(ZSU), prv(ZGA), prv(ZGG), nxt(ZSC), nxt(ZSU), nxt(ZGA), nxt(ZGG),
                  full((SHORT_CONV_W, BRANCH_W)), full((CONF_CONV_W, BRANCH_W)), row, row, row,
                  pl.BlockSpec(memory_space=pl.ANY)],
        out_specs=pl.BlockSpec((tt, 2 * BRANCH_W), lambda i: (i, 1)),
        out_shape=jax.ShapeDtypeStruct((d.m_all, D_MODEL), BF16),
        scratch_shapes=[pltpu.VMEM((tt + 2 * HALO, BRANCH_W), F32), pltpu.VMEM((tt + 2 * HALO, BRANCH_W), F32)],
        input_output_aliases={18: 0},
        compiler_params=_params(("arbitrary",)),
        name="conv_branches",
    )(z, z, z, z, z, z, z, z, z, z, z, z, z, short_w, dw_w, dw_b.reshape(1, BRANCH_W),
      ln_g.reshape(1, BRANCH_W), ln_b.reshape(1, BRANCH_W), br)


def _regroup_in_proj(w_in):
    return jnp.concatenate([w_in[..., :OFF_K], w_in[..., OFF_HQ:OFF_GATE], w_in[..., OFF_K:OFF_HQ],
                            w_in[..., OFF_GATE:]], axis=-1)


def kernel(x, c, ctx, c_ctx, w_mod, b_mod, norm_g, ffn_w1, ffn_w2, w_in, hgrn_lb_logits, hgrn_norm_g, qk_norm_g,
           short_conv_w, conf_dw_w, conf_dw_b, conf_ln_g, conf_ln_b, w_branch, w_out):
    d = Dims(nb=x.shape[0], t_lat=x.shape[1], t_ctx=ctx.shape[1])
    assert d.nb + 1 <= MOD_ROWS and d.t_lat % d.t_ctx == 0 and d.t_ctx % HGRN_CHUNK == 0 and d.t_ctx % HALO == 0
    assert d.t_lat % GRID_W == 0 and d.t_ctx >= HALO

    cc = jnp.concatenate([c, c_ctx[None], jnp.zeros((MOD_ROWS - d.nb - 1, D_MODEL), F32)], axis=0)
    modv_all = mod_vectors(cc, w_mod, b_mod).reshape(DEPTH, MOD_ROWS, 1, N_MOD * D_MODEL)

    s = jnp.concatenate([ctx, x], axis=1).reshape(d.m_all, D_MODEL)
    w1_bf = ffn_w1.astype(BF16)
    w2_bf = ffn_w2.astype(BF16)
    w_in_bf = _regroup_in_proj(w_in).astype(BF16)
    w_branch_bf = w_branch.astype(BF16)
    w_out_bf = w_out.astype(BF16)
    gla_consts = _gla_constants()
    tables = _rope_tables(d)

    for l in range(DEPTH):
        last = l == DEPTH - 1
        modv = modv_all[l]
        g6 = norm_g[l].reshape(6, 1, D_MODEL)

        h = normmod(d, s, g6, modv, 0, 0)
        y = matmul(swiglu_up(h, w1_bf[l, 0]), w2_bf[l, 0], D_MODEL, 1024, BF16)
        s, h = resid_normmod(d, s, y, g6, modv, 1, 2, FFN_RESIDUAL, 2, 3)

        z = matmul(h, w_in_bf[l], Z_COLS, 512, F32)
        br = hgrn_branch(d, z, hgrn_lb_logits, hgrn_norm_g[l], l, gla_consts)
        br = attention_branch(d, z, br, qk_norm_g[l], tables)
        br = conv_branches(d, z, br, short_conv_w[l], conf_dw_w[l], conf_dw_b[l], conf_ln_g[l], conf_ln_b[l])
        y = matmul(gated_merge(h, br, w_in_bf[l], w_branch_bf[l]), w_out_bf[l], D_MODEL, 1024, BF16)

        if last:
            s, h = resid_normmod_latent(d, s, y, g6, modv, 3, 5, 1.0, 4, 6)
        else:
            s, h = resid_normmod(d, s, y, g6, modv, 3, 5, 1.0, 4, 6)
        y = matmul(swiglu_up(h, w1_bf[l, 1]), w2_bf[l, 1], D_MODEL, 1024, BF16)
        s = resid_last(d, s, y, g6, modv, 5, 8, FFN_RESIDUAL, dense_latent=last)
    return s.reshape(d.nb, d.t_lat, D_MODEL)
```

```python
import functools
from typing import NamedTuple

import numpy as np
import jax
import jax.numpy as jnp
from jax import lax
from jax.experimental import pallas as pl
from jax.experimental.pallas import tpu as pltpu

D_MODEL = 4096
DEPTH = 2
GRID_W = 64
N_BRANCH = 4
BRANCH_W = D_MODEL // 4
HGRN_DK = 128
HGRN_DV = 128
HGRN_HEADS = BRANCH_W // HGRN_DV
HGRN_CHUNK = 64
HEAD_DIM = 128
ATTN_Q_HEADS = BRANCH_W // HEAD_DIM
ATTN_KV_HEADS = ATTN_Q_HEADS // 4
ATTN_GROUP = ATTN_Q_HEADS // ATTN_KV_HEADS
ROPE_AXIS_DIM = HEAD_DIM // 2
ROPE_THETA = 10000.0
SHORT_CONV_W = 3
CONF_CONV_W = 31
FFN_DIM = D_MODEL
FFN_RESIDUAL = 0.5
N_MOD = 9
EPS = 1e-6
LB_FLOOR = 1e-30
LOG2_E = 1.4426950408889634

HK = HGRN_HEADS * HGRN_DK
HV = HGRN_HEADS * HGRN_DV
AQ = ATTN_Q_HEADS * HEAD_DIM
AKV = ATTN_KV_HEADS * HEAD_DIM
OFF_K = 2 * HK + HV
OFF_HQ = OFF_K + 2 * AKV
OFF_GATE = OFF_HQ + HK + HV + AQ + 3 * BRANCH_W + 2 * BRANCH_W
N_IN_COLS = OFF_GATE + N_BRANCH * D_MODEL

ZF, ZB, ZI, ZHQ, ZHG, ZAQ, ZSB, ZSC, ZSU, ZGA, ZGG = (k * BRANCH_W for k in range(11))
ZK = 11 * BRANCH_W
ZV = ZK + AKV
Z_COLS = ZV + AKV
assert Z_COLS == OFF_GATE

MOD_ROWS = 16
HALO = 16
assert CONF_CONV_W // 2 < HALO and HALO % 8 == 0
LANES = 128

V7X_VMEM_BYTES = 64 * 1024 * 1024
VMEM_LIMIT = V7X_VMEM_BYTES - 8 * 1024 * 1024

BF16 = jnp.bfloat16
F32 = jnp.float32


class Dims(NamedTuple):
    nb: int
    t_lat: int
    t_ctx: int

    @property
    def pb(self):
        return self.t_lat + self.t_ctx

    @property
    def m_all(self):
        return self.nb * self.pb

    @property
    def tiles_per_batch(self):
        return self.pb // self.t_ctx


def _params(sem):
    return pltpu.CompilerParams(dimension_semantics=sem, vmem_limit_bytes=VMEM_LIMIT)


def _mod_kernel(cc_ref, w_ref, b_ref, o_ref):
    a = cc_ref[...]
    a = (a * jax.nn.sigmoid(a)).astype(BF16)
    o_ref[...] = jnp.dot(a, w_ref[...].astype(BF16), preferred_element_type=F32) + b_ref[...]


def mod_vectors(cc, w_mod, b_mod):
    tn = 512
    n = N_MOD * D_MODEL
    return pl.pallas_call(
        _mod_kernel,
        grid=(DEPTH, n // tn),
        in_specs=[pl.BlockSpec((MOD_ROWS, D_MODEL), lambda l, j: (0, 0)),
                  pl.BlockSpec((None, D_MODEL, tn), lambda l, j: (l, 0, j)),
                  pl.BlockSpec((None, 1, tn), lambda l, j: (l, 0, j))],
        out_specs=pl.BlockSpec((None, MOD_ROWS, tn), lambda l, j: (l, 0, j)),
        out_shape=jax.ShapeDtypeStruct((DEPTH, MOD_ROWS, n), F32),
        compiler_params=_params(("arbitrary", "arbitrary")),
        name="mod_vectors",
    )(cc, w_mod, b_mod.reshape(DEPTH, 1, n))


def _mod_spec(d, chunk):
    tpb = d.tiles_per_batch
    return pl.BlockSpec((None, 1, D_MODEL), lambda i: (jnp.where(i % tpb == 0, d.nb, i // tpb), 0, chunk))


def _g_spec(k):
    return pl.BlockSpec((None, 1, D_MODEL), lambda i: (k, 0, 0))


def _row_spec(d):
    return pl.BlockSpec((d.t_ctx, D_MODEL), lambda i: (i, 0))


def _lat_in_spec(d, chunk=None):
    tpb = d.tiles_per_batch
    if chunk is None:
        return pl.BlockSpec((d.t_ctx, D_MODEL), lambda b, j: (b * tpb + 1 + j, 0))
    return pl.BlockSpec((None, 1, D_MODEL), lambda b, j: (b, 0, chunk))


def _rms(x):
    return x * lax.rsqrt(jnp.mean(x * x, axis=-1, keepdims=True) + EPS)


def _normmod_kernel(s_ref, g_ref, sh_ref, sc_ref, o_ref):
    y = _rms(s_ref[...]) * g_ref[...]
    o_ref[...] = (y * (1 + sc_ref[...]) + sh_ref[...]).astype(o_ref.dtype)


def normmod(d, s, g6, modv, g_idx, shift_idx):
    return pl.pallas_call(
        _normmod_kernel,
        grid=(d.m_all // d.t_ctx,),
        in_specs=[_row_spec(d), _g_spec(g_idx), _mod_spec(d, shift_idx), _mod_spec(d, shift_idx + 1)],
        out_specs=_row_spec(d),
        out_shape=jax.ShapeDtypeStruct((d.m_all, D_MODEL), BF16),
        compiler_params=_params(("arbitrary",)),
        name="normmod",
    )(s, g6, modv, modv)


def _resid_kernel(s_ref, y_ref, gpost_ref, gate_ref, gpre_ref, sh_ref, sc_ref, so_ref, ho_ref, *, coef):
    yn = _rms(y_ref[...].astype(F32)) * gpost_ref[...]
    s = s_ref[...] + (coef * gate_ref[...]) * yn
    so_ref[...] = s
    h = _rms(s) * gpre_ref[...]
    ho_ref[...] = (h * (1 + sc_ref[...]) + sh_ref[...]).astype(ho_ref.dtype)


def _resid_last_kernel(s_ref, y_ref, gpost_ref, gate_ref, so_ref, *, coef):
    yn = _rms(y_ref[...].astype(F32)) * gpost_ref[...]
    so_ref[...] = s_ref[...] + (coef * gate_ref[...]) * yn


def resid_normmod(d, s, y, g6, modv, gpost_idx, gate_idx, coef, gpre_idx, shift_idx, g6_pre=None, modv_pre=None):
    g6_pre = g6 if g6_pre is None else g6_pre
    modv_pre = modv if modv_pre is None else modv_pre
    return pl.pallas_call(
        functools.partial(_resid_kernel, coef=coef),
        grid=(d.m_all // d.t_ctx,),
        in_specs=[_row_spec(d), _row_spec(d), _g_spec(gpost_idx), _mod_spec(d, gate_idx),
                  _g_spec(gpre_idx), _mod_spec(d, shift_idx), _mod_spec(d, shift_idx + 1)],
        out_specs=[_row_spec(d), _row_spec(d)],
        out_shape=[jax.ShapeDtypeStruct((d.m_all, D_MODEL), F32), jax.ShapeDtypeStruct((d.m_all, D_MODEL), BF16)],
        compiler_params=_params(("arbitrary",)),
        name="resid_normmod",
    )(s, y, g6, modv, g6_pre, modv_pre, modv_pre)


def resid_normmod_latent(d, s, y, g6, modv, gpost_idx, gate_idx, coef, gpre_idx, shift_idx):
    nl = d.t_lat // d.t_ctx
    g_spec = lambda k: pl.BlockSpec((None, 1, D_MODEL), lambda b, j: (k, 0, 0))
    dense = pl.BlockSpec((d.t_ctx, D_MODEL), lambda b, j: (b * nl + j, 0))
    return pl.pallas_call(
        functools.partial(_resid_kernel, coef=coef),
        grid=(d.nb, nl),
        in_specs=[_lat_in_spec(d), _lat_in_spec(d), g_spec(gpost_idx), _lat_in_spec(d, gate_idx),
                  g_spec(gpre_idx), _lat_in_spec(d, shift_idx), _lat_in_spec(d, shift_idx + 1)],
        out_specs=[dense, dense],
        out_shape=[jax.ShapeDtypeStruct((d.nb * d.t_lat, D_MODEL), F32),
                   jax.ShapeDtypeStruct((d.nb * d.t_lat, D_MODEL), BF16)],
        compiler_params=_params(("arbitrary", "arbitrary")),
        name="resid_normmod_latent",
    )(s, y, g6, modv, g6, modv, modv)


def resid_last_latent(d, s, y, g6, modv, gpost_idx, gate_idx, coef):
    nl = d.t_lat // d.t_ctx
    dense = pl.BlockSpec((d.t_ctx, D_MODEL), lambda b, j: (b * nl + j, 0))
    return pl.pallas_call(
        functools.partial(_resid_last_kernel, coef=coef),
        grid=(d.nb, nl),
        in_specs=[dense, dense, pl.BlockSpec((None, 1, D_MODEL), lambda b, j: (gpost_idx, 0, 0)),
                  pl.BlockSpec((None, 1, D_MODEL), lambda b, j: (b, 0, gate_idx))],
        out_specs=dense,
        out_shape=jax.ShapeDtypeStruct((d.nb * d.t_lat, D_MODEL), F32),
        compiler_params=_params(("arbitrary", "arbitrary")),
        name="resid_last_latent",
    )(s, y, g6, modv)


def _row_tile(m):
    return next(t for t in (1024, 512, 256, 128) if m % t == 0)


def _mm_kernel(x_ref, w_ref, o_ref):
    o_ref[...] = jnp.dot(x_ref[...], w_ref[...], preferred_element_type=F32).astype(o_ref.dtype)


def matmul(x, w, ncols, tn, out_dtype, out_col_block=lambda j: j):
    m, k = x.shape
    tm = _row_tile(m)
    return pl.pallas_call(
        _mm_kernel,
        grid=(m // tm, ncols // tn),
        in_specs=[pl.BlockSpec((tm, k), lambda i, j: (i, 0)),
                  pl.BlockSpec((k, tn), lambda i, j: (0, j))],
        out_specs=pl.BlockSpec((tm, tn), lambda i, j: (i, out_col_block(j))),
        out_shape=jax.ShapeDtypeStruct((m, ncols), out_dtype),
        compiler_params=_params(("arbitrary", "arbitrary")),
        name="matmul",
    )(x, w)


def _swiglu_kernel(x_ref, wa_ref, wb_ref, o_ref):
    x = x_ref[...]
    a = jnp.dot(x, wa_ref[...], preferred_element_type=F32)
    b = jnp.dot(x, wb_ref[...], preferred_element_type=F32)
    o_ref[...] = (a * jax.nn.sigmoid(a) * b).astype(o_ref.dtype)


def swiglu_up(h, w1):
    m = h.shape[0]
    tm = _row_tile(m)
    tn = 512
    nb = FFN_DIM // tn
    return pl.pallas_call(
        _swiglu_kernel,
        grid=(m // tm, nb),
        in_specs=[pl.BlockSpec((tm, D_MODEL), lambda i, j: (i, 0)),
                  pl.BlockSpec((D_MODEL, tn), lambda i, j: (0, j)),
                  pl.BlockSpec((D_MODEL, tn), lambda i, j: (0, j + nb))],
        out_specs=pl.BlockSpec((tm, tn), lambda i, j: (i, j)),
        out_shape=jax.ShapeDtypeStruct((m, FFN_DIM), BF16),
        compiler_params=_params(("arbitrary", "arbitrary")),
        name="swiglu_up",
    )(h, w1, w1)


def _merge_kernel(h_ref, br_ref, wg_ref, wb_ref, o_ref, acc_ref):
    b = pl.program_id(2)
    g = jnp.dot(h_ref[...], wg_ref[...], preferred_element_type=F32)
    p = jnp.dot(br_ref[...], wb_ref[...], preferred_element_type=F32)
    term = jax.nn.sigmoid(g) * p

    @pl.when(b == 0)
    def _():
        acc_ref[...] = term

    @pl.when(b > 0)
    def _():
        acc_ref[...] += term

    @pl.when(b == N_BRANCH - 1)
    def _():
        o_ref[...] = acc_ref[...].astype(o_ref.dtype)


def gated_merge(h, br, w_in, w_branch):
    m = h.shape[0]
    tm = _row_tile(m)
    tn = 512
    nb = D_MODEL // tn
    gate0 = OFF_GATE // tn
    return pl.pallas_call(
        _merge_kernel,
        grid=(m // tm, nb, N_BRANCH),
        in_specs=[pl.BlockSpec((tm, D_MODEL), lambda i, j, b: (i, 0)),
                  pl.BlockSpec((tm, BRANCH_W), lambda i, j, b: (i, b)),
                  pl.BlockSpec((D_MODEL, tn), lambda i, j, b: (0, gate0 + b * nb + j)),
                  pl.BlockSpec((None, BRANCH_W, tn), lambda i, j, b: (b, 0, j))],
        out_specs=pl.BlockSpec((tm, tn), lambda i, j, b: (i, j)),
        out_shape=jax.ShapeDtypeStruct((m, D_MODEL), BF16),
        scratch_shapes=[pltpu.VMEM((tm, tn), F32)],
        compiler_params=_params(("arbitrary", "arbitrary", "arbitrary")),
        name="gated_merge",
    )(h, br, w_in, w_branch)


_GLA_LEVELS = (8, 16, 32)
GLA_HEADS_PER_STEP = 2


def _gla_constants():
    c = HGRN_CHUNK
    i = np.arange(c)[:, None]
    t = np.arange(c)[None, :]
    flip = lambda m: m[::-1, ::-1]
    mats = [t <= i]
    masks = []
    for s in _GLA_LEVELS:
        blk = i // s
        right = blk % 2 == 1
        mats.append(right & (t > blk * s) & (t <= i))
        mats.append(~right & (t > i) & (t <= (blk + 1) * s))
        masks.append(right & (t // s == blk - 1))
    mats.append(t > i)
    g_f = np.concatenate(mats, 0).astype(np.float32)
    g_b = np.concatenate([flip(m) for m in mats], 0).astype(np.float32)
    m_f = np.stack(masks).astype(np.float32)
    m_b = np.stack([flip(m) for m in masks]).astype(np.float32)
    tile3 = lambda g: jnp.asarray(np.tile(g, (1, 3)), BF16)
    return tile3(g_f), tile3(g_b), jnp.asarray(m_f), jnp.asarray(m_b)


def _nt_dot(a, b):
    return lax.dot_general(a, b, (((1,), (1,)), ((), ())), preferred_element_type=F32)


def _gla_chunk(a, q, v, lb_floor, one_minus_lb, gmat, masks, st, reverse):
    c = HGRN_CHUNK
    nsub = c // 8
    g_rows = (2 + 2 * len(_GLA_LEVELS)) * c
    f = lb_floor + one_minus_lb * jax.nn.sigmoid(a)
    lf = jnp.log(f)
    k = 1.0 - f
    hi = lf.astype(BF16)
    r1 = lf - hi.astype(F32)
    mid = r1.astype(BF16)
    lo = (r1 - mid.astype(F32)).astype(BF16)
    ex = jnp.dot(gmat, jnp.concatenate([hi, mid, lo], axis=0), preferred_element_type=F32)
    cum = ex[0:c]
    vb16 = v.astype(BF16)

    p = jnp.zeros((c, c), F32)
    for li in range(len(_GLA_LEVELS)):
        ea = ex[c * (1 + 2 * li):c * (2 + 2 * li)]
        eb = ex[c * (2 + 2 * li):c * (3 + 2 * li)]
        qa = (q * jnp.exp(jnp.minimum(ea, 0.0))).astype(BF16)
        kb = (k * jnp.exp(jnp.minimum(eb, 0.0))).astype(BF16)
        p = p + masks[li] * _nt_dot(qa, kb)
    o = jnp.dot(p.astype(BF16), vb16, preferred_element_type=F32)

    q3 = q.reshape(nsub, 8, LANES)
    k3 = k.reshape(nsub, 8, LANES)
    c3 = cum.reshape(nsub, 8, LANES)
    v3 = v.reshape(nsub, 8, LANES)
    ii = lax.broadcasted_iota(jnp.int32, (nsub, 8, LANES), 1)
    ws = []
    for jj in range(8):
        kj = jnp.broadcast_to(k3[:, jj:jj + 1, :], (nsub, 8, LANES))
        cj = jnp.broadcast_to(c3[:, jj:jj + 1, :], (nsub, 8, LANES))
        w = q3 * kj * jnp.exp(jnp.minimum(c3 - cj, 0.0))
        keep = (ii <= jj) if reverse else (ii >= jj)
        ws.append(jnp.where(keep, w, 0.0).reshape(c, LANES).astype(BF16))
    rs = jnp.dot(jnp.concatenate(ws, axis=0), jnp.ones((LANES, LANES), BF16), preferred_element_type=F32)
    o3 = o.reshape(nsub, 8, LANES)
    for jj in range(8):
        vj = jnp.broadcast_to(v3[:, jj:jj + 1, :], (nsub, 8, LANES))
        o3 = o3 + rs[jj * c:(jj + 1) * c].reshape(nsub, 8, LANES) * vj
    o = o3.reshape(c, LANES)

    o = o + _nt_dot((q * jnp.exp(cum)).astype(BF16), st.astype(BF16))
    ke = (k * jnp.exp(ex[g_rows - c:g_rows])).astype(BF16)
    total = cum[0:1] if reverse else cum[c - 1:c]
    st_new = jnp.exp(total) * st + jnp.dot(v.T.astype(BF16), ke, preferred_element_type=F32)
    return o, st_new


def _gla_kernel(af_ref, ab_ref, v_ref, q_ref, g_ref, lbl_ref, ng_ref, gf_ref, gb_ref, mf_ref, mb_ref,
                o_ref, of_ref, ob_ref, sf_ref, sb_ref, *, layer, t_ctx, t_lat):
    c = HGRN_CHUNK
    logits = lbl_ref[...]
    e = jnp.exp(logits - jnp.max(logits, axis=0, keepdims=True))
    pr = e / jnp.sum(e, axis=0, keepdims=True)
    csum = pr[0]
    for dd in range(1, layer + 1):
        csum = csum + pr[dd]
    lb = csum - pr[0]
    lb_floor = jnp.maximum(lb, LB_FLOOR)
    one_minus_lb = 1.0 - lb
    masks_f = [mf_ref[li] for li in range(len(_GLA_LEVELS))]
    masks_b = [mb_ref[li] for li in range(len(_GLA_LEVELS))]
    gmat_f = gf_ref[...]
    gmat_b = gb_ref[...]

    sf_ref[...] = jnp.zeros_like(sf_ref)
    sb_ref[...] = jnp.zeros_like(sb_ref)

    def scan(row0, nchunks):
        def body(ci, carry):
            rf = pl.multiple_of(row0 + ci * c, c)
            rb = pl.multiple_of(row0 + (nchunks - 1 - ci) * c, c)
            lanes = [slice(hh * LANES, (hh + 1) * LANES) for hh in range(GLA_HEADS_PER_STEP)]
            ins_f = [(af_ref[pl.ds(rf, c), ln], q_ref[pl.ds(rf, c), ln], v_ref[pl.ds(rf, c), ln]) for ln in lanes]
            ins_b = [(ab_ref[pl.ds(rb, c), ln], q_ref[pl.ds(rb, c), ln], v_ref[pl.ds(rb, c), ln]) for ln in lanes]
            st_f = [sf_ref[hh] for hh in range(GLA_HEADS_PER_STEP)]
            st_b = [sb_ref[hh] for hh in range(GLA_HEADS_PER_STEP)]
            outs = []
            for hh, ln in enumerate(lanes):
                outs.append(_gla_chunk(*ins_f[hh], lb_floor[0:1, ln], one_minus_lb[0:1, ln], gmat_f, masks_f,
                                       st_f[hh], False))
                outs.append(_gla_chunk(*ins_b[hh], lb_floor[1:2, ln], one_minus_lb[1:2, ln], gmat_b, masks_b,
                                       st_b[hh], True))
            for hh, ln in enumerate(lanes):
                (o_f, s_f), (o_b, s_b) = outs[2 * hh], outs[2 * hh + 1]
                of_ref[pl.ds(rf, c), ln] = o_f
                ob_ref[pl.ds(rb, c), ln] = o_b
                sf_ref[hh] = s_f
                sb_ref[hh] = s_b
            return carry
        lax.fori_loop(0, nchunks, body, 0)

    scan(0, t_ctx // c)
    scan(t_ctx, t_lat // c)

    for hh in range(GLA_HEADS_PER_STEP):
        ln = slice(hh * LANES, (hh + 1) * LANES)
        o = of_ref[:, ln] + ob_ref[:, ln]
        y = _rms(o) * ng_ref[...] * jax.nn.sigmoid(g_ref[:, ln])
        o_ref[:, ln] = y.astype(o_ref.dtype)


def hgrn_branch(d, z, lb_logits, norm_g, layer, consts):
    g_f, g_b, m_f, m_b = consts
    w = GLA_HEADS_PER_STEP * LANES
    col = lambda base: (lambda b, h: (b, base // w + h))
    zspec = lambda base: pl.BlockSpec((d.pb, w), col(base))
    const2 = lambda shape: pl.BlockSpec(shape, lambda b, h: (0,) * len(shape))
    state = pltpu.VMEM((GLA_HEADS_PER_STEP, HGRN_DV, HGRN_DK), F32)
    return pl.pallas_call(
        functools.partial(_gla_kernel, layer=layer, t_ctx=d.t_ctx, t_lat=d.t_lat),
        grid=(d.nb, HGRN_HEADS // GLA_HEADS_PER_STEP),
        in_specs=[zspec(ZF), zspec(ZB), zspec(ZI), zspec(ZHQ), zspec(ZHG),
                  pl.BlockSpec((DEPTH, 2, w), lambda b, h: (0, 0, h)),
                  const2((1, HGRN_DV)), const2(g_f.shape), const2(g_b.shape), const2(m_f.shape), const2(m_b.shape)],
        out_specs=pl.BlockSpec((d.pb, w), lambda b, h: (b, h)),
        out_shape=jax.ShapeDtypeStruct((d.m_all, D_MODEL), BF16),
        scratch_shapes=[pltpu.VMEM((d.pb, w), F32), pltpu.VMEM((d.pb, w), F32), state, state],
        compiler_params=_params(("arbitrary", "arbitrary")),
        name="hgrn_branch",
    )(z, z, z, z, z, lb_logits, norm_g.reshape(1, HGRN_DV), g_f, g_b, m_f, m_b)


def _rope_tables(d):
    half = ROPE_AXIS_DIM // 2
    tok = np.arange(d.t_lat)
    inv = ROPE_THETA ** (-np.arange(0, ROPE_AXIS_DIM, 2, dtype=np.float32) / ROPE_AXIS_DIM)
    ang_r = (tok // GRID_W).astype(np.float32)[:, None] * inv.astype(np.float32)
    ang_c = (tok % GRID_W).astype(np.float32)[:, None] * inv.astype(np.float32)
    ang = jnp.asarray(np.concatenate([ang_r, ang_r, ang_c, ang_c], axis=1), F32)
    first = jnp.asarray((np.arange(HEAD_DIM) % ROPE_AXIS_DIM) < half)
    cos = jnp.cos(ang)
    sin = jnp.sin(ang)
    sin_a = jnp.where(first, -sin, 0.0)
    sin_b = jnp.where(first, 0.0, sin)
    pad = lambda t, fill: jnp.concatenate([jnp.full((d.t_ctx, HEAD_DIM), fill, F32), t], axis=0)
    return pad(cos, 1.0), pad(sin_a, 0.0), pad(sin_b, 0.0)


def _rope(u, cos, sin_a, sin_b):
    half = ROPE_AXIS_DIM // 2
    return u * cos + pltpu.roll(u, HEAD_DIM - half, 1) * sin_a + pltpu.roll(u, half, 1) * sin_b


def _attn_kernel(q_ref, k_ref, v_ref, g_ref, ck_ref, sak_ref, sbk_ref, cq_ref, saq_ref, sbq_ref, br_ref,
                 o_ref, kbuf, vbuf, *, t_ctx):
    del br_ref
    qi = pl.program_id(2)
    tq = q_ref.shape[0]
    scale = HEAD_DIM ** -0.5

    @pl.when(qi == 0)
    def _():
        kn = _rms(k_ref[...]) * g_ref[1:2, :]
        kbuf[...] = _rope(kn, ck_ref[...], sak_ref[...], sbk_ref[...]).astype(BF16)
        vbuf[...] = v_ref[...].astype(BF16)

    q = q_ref[...]
    cq, saq, sbq = cq_ref[...], saq_ref[...], sbq_ref[...]
    qs = []
    for gi in range(ATTN_GROUP):
        qn = _rms(q[:, gi * HEAD_DIM:(gi + 1) * HEAD_DIM]) * g_ref[0:1, :]
        qs.append(_rope(qn, cq, saq, sbq).astype(BF16))
    q4 = jnp.concatenate(qs, axis=0)

    def attend(keys, vals):
        s = _nt_dot(q4, keys)
        m = jnp.max(s, axis=-1, keepdims=True)
        p = jnp.exp2((s - m) * (scale * LOG2_E))
        l = jnp.sum(p, axis=-1, keepdims=True)
        o = jnp.dot(p.astype(BF16), vals, preferred_element_type=F32) / l
        for gi in range(ATTN_GROUP):
            o_ref[:, gi * HEAD_DIM:(gi + 1) * HEAD_DIM] = o[gi * tq:(gi + 1) * tq].astype(o_ref.dtype)

    @pl.when(qi == 0)
    def _():
        attend(kbuf[0:t_ctx, :], vbuf[0:t_ctx, :])

    @pl.when(qi > 0)
    def _():
        attend(kbuf[...], vbuf[...])


def attention_branch(d, z, br, qk_norm_g, tables):
    cos, sin_a, sin_b = tables
    tq = d.t_ctx
    nq = d.tiles_per_batch
    gw = ATTN_GROUP * HEAD_DIM
    kspec = lambda base: pl.BlockSpec((d.pb, HEAD_DIM), lambda b, h, qi: (b, base // HEAD_DIM + h))
    tab_k = pl.BlockSpec((d.pb, HEAD_DIM), lambda b, h, qi: (0, 0))
    tab_q = pl.BlockSpec((tq, HEAD_DIM), lambda b, h, qi: (qi, 0))
    return pl.pallas_call(
        functools.partial(_attn_kernel, t_ctx=d.t_ctx),
        grid=(d.nb, ATTN_KV_HEADS, nq),
        in_specs=[pl.BlockSpec((tq, gw), lambda b, h, qi: (b * nq + qi, ZAQ // gw + h)),
                  kspec(ZK), kspec(ZV),
                  pl.BlockSpec((2, HEAD_DIM), lambda b, h, qi: (0, 0)),
                  tab_k, tab_k, tab_k, tab_q, tab_q, tab_q,
                  pl.BlockSpec(memory_space=pl.ANY)],
        out_specs=pl.BlockSpec((tq, gw), lambda b, h, qi: (b * nq + qi, BRANCH_W // gw + h)),
        out_shape=jax.ShapeDtypeStruct((d.m_all, D_MODEL), BF16),
        scratch_shapes=[pltpu.VMEM((d.pb, HEAD_DIM), BF16), pltpu.VMEM((d.pb, HEAD_DIM), BF16)],
        input_output_aliases={10: 0},
        compiler_params=_params(("arbitrary", "arbitrary", "arbitrary")),
        name="attention_branch",
    )(z, z, z, qk_norm_g, cos, sin_a, sin_b, cos, sin_a, sin_b, br)


def _conv_kernel(sb_ref, sc_ref, su_ref, ga_ref, gg_ref,
                 scp_ref, sup_ref, gap_ref, ggp_ref, scn_ref, sun_ref, gan_ref, ggn_ref,
                 ws_ref, wc_ref, bc_ref, lg_ref, lb_ref, br_ref, o_ref, ps_ref, pc_ref, *, tiles_per_batch):
    del br_ref
    tt = sb_ref.shape[0]
    within = pl.program_id(0) % tiles_per_batch
    prev_ok = (within >= 2).astype(F32)
    next_ok = jnp.logical_and(within >= 1, within <= tiles_per_batch - 2).astype(F32)

    glu = lambda a, g: a * jax.nn.sigmoid(g)
    ps_ref[0:HALO, :] = scp_ref[...] * sup_ref[...] * prev_ok
    ps_ref[HALO:HALO + tt, :] = sc_ref[...] * su_ref[...]
    ps_ref[HALO + tt:2 * HALO + tt, :] = scn_ref[...] * sun_ref[...] * next_ok
    pc_ref[0:HALO, :] = glu(gap_ref[...], ggp_ref[...]) * prev_ok
    pc_ref[HALO:HALO + tt, :] = glu(ga_ref[...], gg_ref[...])
    pc_ref[HALO + tt:2 * HALO + tt, :] = glu(gan_ref[...], ggn_ref[...]) * next_ok

    acc = jnp.zeros((tt, BRANCH_W), F32)
    for tau in range(SHORT_CONV_W):
        acc = acc + ps_ref[pl.ds(HALO - SHORT_CONV_W // 2 + tau, tt), :] * ws_ref[tau:tau + 1, :]
    o_ref[:, 0:BRANCH_W] = (sb_ref[...] * acc).astype(o_ref.dtype)

    acc = jnp.zeros((tt, BRANCH_W), F32)
    for tau in range(CONF_CONV_W):
        acc = acc + pc_ref[pl.ds(HALO - CONF_CONV_W // 2 + tau, tt), :] * wc_ref[tau:tau + 1, :]
    u = acc + bc_ref[...]
    uc = u - jnp.mean(u, axis=-1, keepdims=True)
    y = uc * lax.rsqrt(jnp.mean(uc * uc, axis=-1, keepdims=True) + EPS) * lg_ref[...] + lb_ref[...]
    o_ref[:, BRANCH_W:2 * BRANCH_W] = (y * jax.nn.sigmoid(y)).astype(o_ref.dtype)


def conv_branches(d, z, br, short_w, dw_w, dw_b, ln_g, ln_b):
    tt = d.t_ctx
    per = tt // HALO
    last = d.m_all // HALO - 1
    cur = lambda base: pl.BlockSpec((tt, BRANCH_W), lambda i: (i, base // BRANCH_W))
    prv = lambda base: pl.BlockSpec((HALO, BRANCH_W), lambda i: (jnp.maximum(i * per - 1, 0), base // BRANCH_W))
    nxt = lambda base: pl.BlockSpec((HALO, BRANCH_W), lambda i: (jnp.minimum((i + 1) * per, last), base // BRANCH_W))
    full = lambda shape: pl.BlockSpec(shape, lambda i: (0,) * len(shape))
    row = full((1, BRANCH_W))
    return pl.pallas_call(
        functools.partial(_conv_kernel, tiles_per_batch=d.tiles_per_batch),
        grid=(d.m_all // tt,),
        in_specs=[cur(ZSB), cur(ZSC), cur(ZSU), cur(ZGA), cur(ZGG),
                  prv(ZSC), prv(ZSU), prv(ZGA), prv(ZGG), nxt(ZSC), nxt(ZSU), nxt(ZGA), nxt(ZGG),
                  full((SHORT_CONV_W, BRANCH_W)), full((CONF_CONV_W, BRANCH_W)), row, row, row,
                  pl.BlockSpec(memory_space=pl.ANY)],
        out_specs=pl.BlockSpec((tt, 2 * BRANCH_W), lambda i: (i, 1)),
        out_shape=jax.ShapeDtypeStruct((d.m_all, D_MODEL), BF16),
        scratch_shapes=[pltpu.VMEM((tt + 2 * HALO, BRANCH_W), F32), pltpu.VMEM((tt + 2 * HALO, BRANCH_W), F32)],
        input_output_aliases={18: 0},
        compiler_params=_params(("arbitrary",)),
        name="conv_branches",
    )(z, z, z, z, z, z, z, z, z, z, z, z, z, short_w, dw_w, dw_b.reshape(1, BRANCH_W),
      ln_g.reshape(1, BRANCH_W), ln_b.reshape(1, BRANCH_W), br)


Z_TN = 2 * AKV
assert OFF_K % Z_TN == 0 and OFF_HQ - OFF_K == Z_TN and Z_COLS % Z_TN == 0


def _z_col_block(j):
    kv = OFF_K // Z_TN
    return jnp.where(j < kv, j, jnp.where(j == kv, Z_COLS // Z_TN - 1, j - 1))


def kernel(x, c, ctx, c_ctx, w_mod, b_mod, norm_g, ffn_w1, ffn_w2, w_in, hgrn_lb_logits, hgrn_norm_g, qk_norm_g,
           short_conv_w, conf_dw_w, conf_dw_b, conf_ln_g, conf_ln_b, w_branch, w_out):
    d = Dims(nb=x.shape[0], t_lat=x.shape[1], t_ctx=ctx.shape[1])
    assert d.nb + 1 <= MOD_ROWS and d.t_lat % d.t_ctx == 0 and d.t_ctx % HGRN_CHUNK == 0 and d.t_ctx % HALO == 0
    assert d.t_lat % GRID_W == 0 and d.t_ctx >= HALO

    cc = jnp.concatenate([c, c_ctx[None], jnp.zeros((MOD_ROWS - d.nb - 1, D_MODEL), F32)], axis=0)
    modv_all = mod_vectors(cc, w_mod, b_mod).reshape(DEPTH, MOD_ROWS, 1, N_MOD * D_MODEL)

    s = jnp.concatenate([ctx, x], axis=1).reshape(d.m_all, D_MODEL)
    w1_bf = ffn_w1.astype(BF16)
    w2_bf = ffn_w2.astype(BF16)
    w_in_bf = w_in.astype(BF16)
    w_branch_bf = w_branch.astype(BF16)
    w_out_bf = w_out.astype(BF16)
    gla_consts = _gla_constants()
    tables = _rope_tables(d)

    g6_all = norm_g.reshape(DEPTH, 6, 1, D_MODEL)
    h = normmod(d, s, g6_all[0], modv_all[0], 0, 0)
    for l in range(DEPTH):
        last = l == DEPTH - 1
        modv = modv_all[l]
        g6 = g6_all[l]

        y = matmul(swiglu_up(h, w1_bf[l, 0]), w2_bf[l, 0], D_MODEL, 1024, BF16)
        s, h = resid_normmod(d, s, y, g6, modv, 1, 2, FFN_RESIDUAL, 2, 3)

        z = matmul(h, w_in_bf[l], Z_COLS, Z_TN, F32, out_col_block=_z_col_block)
        br = hgrn_branch(d, z, hgrn_lb_logits, hgrn_norm_g[l], l, gla_consts)
        br = attention_branch(d, z, br, qk_norm_g[l], tables)
        br = conv_branches(d, z, br, short_conv_w[l], conf_dw_w[l], conf_dw_b[l], conf_ln_g[l], conf_ln_b[l])
        y = matmul(gated_merge(h, br, w_in_bf[l], w_branch_bf[l]), w_out_bf[l], D_MODEL, 1024, BF16)

        if last:
            s, h = resid_normmod_latent(d, s, y, g6, modv, 3, 5, 1.0, 4, 6)
        else:
            s, h = resid_normmod(d, s, y, g6, modv, 3, 5, 1.0, 4, 6)
        y = matmul(swiglu_up(h, w1_bf[l, 1]), w2_bf[l, 1], D_MODEL, 1024, BF16)
        if last:
            s = resid_last_latent(d, s, y, g6, modv, 5, 8, FFN_RESIDUAL)
        else:
            s, h = resid_normmod(d, s, y, g6, modv, 5, 8, FFN_RESIDUAL, 0, 0, g6_all[l + 1], modv_all[l + 1])
    return s.reshape(d.nb, d.t_lat, D_MODEL)
```

```python
import functools
from typing import NamedTuple

import numpy as np
import jax
import jax.numpy as jnp
from jax import lax
from jax.experimental import pallas as pl
from jax.experimental.pallas import tpu as pltpu

D_MODEL = 4096
DEPTH = 2
GRID_W = 64
N_BRANCH = 4
BRANCH_W = D_MODEL // 4
HGRN_DK = 128
HGRN_DV = 128
HGRN_HEADS = BRANCH_W // HGRN_DV
HGRN_CHUNK = 64
HEAD_DIM = 128
ATTN_Q_HEADS = BRANCH_W // HEAD_DIM
ATTN_KV_HEADS = ATTN_Q_HEADS // 4
ATTN_GROUP = ATTN_Q_HEADS // ATTN_KV_HEADS
ROPE_AXIS_DIM = HEAD_DIM // 2
ROPE_THETA = 10000.0
SHORT_CONV_W = 3
CONF_CONV_W = 31
FFN_DIM = D_MODEL
FFN_RESIDUAL = 0.5
N_MOD = 9
EPS = 1e-6
LB_FLOOR = 1e-30
LOG2_E = 1.4426950408889634

HK = HGRN_HEADS * HGRN_DK
HV = HGRN_HEADS * HGRN_DV
AQ = ATTN_Q_HEADS * HEAD_DIM
AKV = ATTN_KV_HEADS * HEAD_DIM
OFF_K = 2 * HK + HV
OFF_HQ = OFF_K + 2 * AKV
OFF_GATE = OFF_HQ + HK + HV + AQ + 3 * BRANCH_W + 2 * BRANCH_W
N_IN_COLS = OFF_GATE + N_BRANCH * D_MODEL

ZF, ZB, ZI, ZHQ, ZHG, ZAQ, ZSB, ZSC, ZSU, ZGA, ZGG = (k * BRANCH_W for k in range(11))
ZK = 11 * BRANCH_W
ZV = ZK + AKV
Z_COLS = ZV + AKV
assert Z_COLS == OFF_GATE

MOD_ROWS = 16
HALO = 16
assert CONF_CONV_W // 2 < HALO and HALO % 8 == 0
LANES = 128

V7X_VMEM_BYTES = 64 * 1024 * 1024
VMEM_LIMIT = V7X_VMEM_BYTES - 8 * 1024 * 1024

BF16 = jnp.bfloat16
F32 = jnp.float32


class Dims(NamedTuple):
    nb: int
    t_lat: int
    t_ctx: int

    @property
    def pb(self):
        return self.t_lat + self.t_ctx

    @property
    def m_all(self):
        return self.nb * self.pb

    @property
    def tiles_per_batch(self):
        return self.pb // self.t_ctx


def _params(sem):
    return pltpu.CompilerParams(dimension_semantics=sem, vmem_limit_bytes=VMEM_LIMIT)


def _mod_kernel(cc_ref, w_ref, b_ref, o_ref):
    a = cc_ref[...]
    a = (a * jax.nn.sigmoid(a)).astype(BF16)
    o_ref[...] = jnp.dot(a, w_ref[...].astype(BF16), preferred_element_type=F32) + b_ref[...]


def mod_vectors(cc, w_mod, b_mod):
    tn = 512
    n = N_MOD * D_MODEL
    return pl.pallas_call(
        _mod_kernel,
        grid=(DEPTH, n // tn),
        in_specs=[pl.BlockSpec((MOD_ROWS, D_MODEL), lambda l, j: (0, 0)),
                  pl.BlockSpec((None, D_MODEL, tn), lambda l, j: (l, 0, j)),
                  pl.BlockSpec((None, 1, tn), lambda l, j: (l, 0, j))],
        out_specs=pl.BlockSpec((None, MOD_ROWS, tn), lambda l, j: (l, 0, j)),
        out_shape=jax.ShapeDtypeStruct((DEPTH, MOD_ROWS, n), F32),
        compiler_params=_params(("arbitrary", "arbitrary")),
        name="mod_vectors",
    )(cc, w_mod, b_mod.reshape(DEPTH, 1, n))


def _mod_spec(d, chunk):
    tpb = d.tiles_per_batch
    return pl.BlockSpec((None, 1, D_MODEL), lambda i: (jnp.where(i % tpb == 0, d.nb, i // tpb), 0, chunk))


def _g_spec(k):
    return pl.BlockSpec((None, 1, D_MODEL), lambda i: (k, 0, 0))


def _row_spec(d):
    return pl.BlockSpec((d.t_ctx, D_MODEL), lambda i: (i, 0))


def _lat_in_spec(d, chunk=None):
    tpb = d.tiles_per_batch
    if chunk is None:
        return pl.BlockSpec((d.t_ctx, D_MODEL), lambda b, j: (b * tpb + 1 + j, 0))
    return pl.BlockSpec((None, 1, D_MODEL), lambda b, j: (b, 0, chunk))


def _rms(x):
    return x * lax.rsqrt(jnp.mean(x * x, axis=-1, keepdims=True) + EPS)


def _normmod_kernel(s_ref, g_ref, sh_ref, sc_ref, o_ref):
    y = _rms(s_ref[...]) * g_ref[...]
    o_ref[...] = (y * (1 + sc_ref[...]) + sh_ref[...]).astype(o_ref.dtype)


def normmod(d, s, g6, modv, g_idx, shift_idx):
    return pl.pallas_call(
        _normmod_kernel,
        grid=(d.m_all // d.t_ctx,),
        in_specs=[_row_spec(d), _g_spec(g_idx), _mod_spec(d, shift_idx), _mod_spec(d, shift_idx + 1)],
        out_specs=_row_spec(d),
        out_shape=jax.ShapeDtypeStruct((d.m_all, D_MODEL), BF16),
        compiler_params=_params(("arbitrary",)),
        name="normmod",
    )(s, g6, modv, modv)


def _resid_kernel(s_ref, y_ref, gpost_ref, gate_ref, gpre_ref, sh_ref, sc_ref, so_ref, ho_ref, *, coef):
    yn = _rms(y_ref[...].astype(F32)) * gpost_ref[...]
    s = s_ref[...] + (coef * gate_ref[...]) * yn
    so_ref[...] = s
    h = _rms(s) * gpre_ref[...]
    ho_ref[...] = (h * (1 + sc_ref[...]) + sh_ref[...]).astype(ho_ref.dtype)


def _resid_last_kernel(s_ref, y_ref, gpost_ref, gate_ref, so_ref, *, coef):
    yn = _rms(y_ref[...].astype(F32)) * gpost_ref[...]
    so_ref[...] = s_ref[...] + (coef * gate_ref[...]) * yn


def resid_normmod(d, s, y, g6, modv, gpost_idx, gate_idx, coef, gpre_idx, shift_idx, g6_pre=None, modv_pre=None):
    g6_pre = g6 if g6_pre is None else g6_pre
    modv_pre = modv if modv_pre is None else modv_pre
    return pl.pallas_call(
        functools.partial(_resid_kernel, coef=coef),
        grid=(d.m_all // d.t_ctx,),
        in_specs=[_row_spec(d), _row_spec(d), _g_spec(gpost_idx), _mod_spec(d, gate_idx),
                  _g_spec(gpre_idx), _mod_spec(d, shift_idx), _mod_spec(d, shift_idx + 1)],
        out_specs=[_row_spec(d), _row_spec(d)],
        out_shape=[jax.ShapeDtypeStruct((d.m_all, D_MODEL), F32), jax.ShapeDtypeStruct((d.m_all, D_MODEL), BF16)],
        compiler_params=_params(("arbitrary",)),
        name="resid_normmod",
    )(s, y, g6, modv, g6_pre, modv_pre, modv_pre)


def resid_normmod_latent(d, s, y, g6, modv, gpost_idx, gate_idx, coef, gpre_idx, shift_idx):
    nl = d.t_lat // d.t_ctx
    g_spec = lambda k: pl.BlockSpec((None, 1, D_MODEL), lambda b, j: (k, 0, 0))
    dense = pl.BlockSpec((d.t_ctx, D_MODEL), lambda b, j: (b * nl + j, 0))
    return pl.pallas_call(
        functools.partial(_resid_kernel, coef=coef),
        grid=(d.nb, nl),
        in_specs=[_lat_in_spec(d), _lat_in_spec(d), g_spec(gpost_idx), _lat_in_spec(d, gate_idx),
                  g_spec(gpre_idx), _lat_in_spec(d, shift_idx), _lat_in_spec(d, shift_idx + 1)],
        out_specs=[dense, dense],
        out_shape=[jax.ShapeDtypeStruct((d.nb * d.t_lat, D_MODEL), F32),
                   jax.ShapeDtypeStruct((d.nb * d.t_lat, D_MODEL), BF16)],
        compiler_params=_params(("arbitrary", "arbitrary")),
        name="resid_normmod_latent",
    )(s, y, g6, modv, g6, modv, modv)


def resid_last_latent(d, s, y, g6, modv, gpost_idx, gate_idx, coef):
    nl = d.t_lat // d.t_ctx
    dense = pl.BlockSpec((d.t_ctx, D_MODEL), lambda b, j: (b * nl + j, 0))
    return pl.pallas_call(
        functools.partial(_resid_last_kernel, coef=coef),
        grid=(d.nb, nl),
        in_specs=[dense, dense, pl.BlockSpec((None, 1, D_MODEL), lambda b, j: (gpost_idx, 0, 0)),
                  pl.BlockSpec((None, 1, D_MODEL), lambda b, j: (b, 0, gate_idx))],
        out_specs=dense,
        out_shape=jax.ShapeDtypeStruct((d.nb * d.t_lat, D_MODEL), F32),
        compiler_params=_params(("arbitrary", "arbitrary")),
        name="resid_last_latent",
    )(s, y, g6, modv)


def _row_tile(m):
    return next(t for t in (1024, 512, 256, 128) if m % t == 0)


def _mm_kernel(x_ref, w_ref, o_ref):
    o_ref[...] = jnp.dot(x_ref[...], w_ref[...], preferred_element_type=F32).astype(o_ref.dtype)


def _w_spec(lead, k, tn, col):
    return pl.BlockSpec((None,) * len(lead) + (k, tn), lambda *g: tuple(lead) + (0, col(*g)))


def matmul(x, w, lead, ncols, tn, out_dtype, out_col_block=lambda j: j):
    m, k = x.shape
    tm = _row_tile(m)
    return pl.pallas_call(
        _mm_kernel,
        grid=(m // tm, ncols // tn),
        in_specs=[pl.BlockSpec((tm, k), lambda i, j: (i, 0)),
                  _w_spec(lead, k, tn, lambda i, j: j)],
        out_specs=pl.BlockSpec((tm, tn), lambda i, j: (i, out_col_block(j))),
        out_shape=jax.ShapeDtypeStruct((m, ncols), out_dtype),
        compiler_params=_params(("arbitrary", "arbitrary")),
        name="matmul",
    )(x, w)


def _swiglu_kernel(x_ref, wa_ref, wb_ref, o_ref):
    x = x_ref[...]
    a = jnp.dot(x, wa_ref[...], preferred_element_type=F32)
    b = jnp.dot(x, wb_ref[...], preferred_element_type=F32)
    o_ref[...] = (a * jax.nn.sigmoid(a) * b).astype(o_ref.dtype)


def swiglu_up(h, w1, lead):
    m = h.shape[0]
    tm = _row_tile(m)
    tn = 512
    nb = FFN_DIM // tn
    return pl.pallas_call(
        _swiglu_kernel,
        grid=(m // tm, nb),
        in_specs=[pl.BlockSpec((tm, D_MODEL), lambda i, j: (i, 0)),
                  _w_spec(lead, D_MODEL, tn, lambda i, j: j),
                  _w_spec(lead, D_MODEL, tn, lambda i, j: j + nb)],
        out_specs=pl.BlockSpec((tm, tn), lambda i, j: (i, j)),
        out_shape=jax.ShapeDtypeStruct((m, FFN_DIM), BF16),
        compiler_params=_params(("arbitrary", "arbitrary")),
        name="swiglu_up",
    )(h, w1, w1)


def _merge_kernel(h_ref, br_ref, wg_ref, wb_ref, o_ref, acc_ref):
    b = pl.program_id(2)
    g = jnp.dot(h_ref[...], wg_ref[...], preferred_element_type=F32)
    p = jnp.dot(br_ref[...], wb_ref[...], preferred_element_type=F32)
    term = jax.nn.sigmoid(g) * p

    @pl.when(b == 0)
    def _():
        acc_ref[...] = term

    @pl.when(b > 0)
    def _():
        acc_ref[...] += term

    @pl.when(b == N_BRANCH - 1)
    def _():
        o_ref[...] = acc_ref[...].astype(o_ref.dtype)


def gated_merge(h, br, w_in, w_branch, layer):
    m = h.shape[0]
    tm = _row_tile(m)
    tn = 512
    nb = D_MODEL // tn
    gate0 = OFF_GATE // tn
    return pl.pallas_call(
        _merge_kernel,
        grid=(m // tm, nb, N_BRANCH),
        in_specs=[pl.BlockSpec((tm, D_MODEL), lambda i, j, b: (i, 0)),
                  pl.BlockSpec((tm, BRANCH_W), lambda i, j, b: (i, b)),
                  _w_spec((layer,), D_MODEL, tn, lambda i, j, b: gate0 + b * nb + j),
                  pl.BlockSpec((None, None, BRANCH_W, tn), lambda i, j, b: (layer, b, 0, j))],
        out_specs=pl.BlockSpec((tm, tn), lambda i, j, b: (i, j)),
        out_shape=jax.ShapeDtypeStruct((m, D_MODEL), BF16),
        scratch_shapes=[pltpu.VMEM((tm, tn), F32)],
        compiler_params=_params(("arbitrary", "arbitrary", "arbitrary")),
        name="gated_merge",
    )(h, br, w_in, w_branch)


_GLA_LEVELS = (8, 16, 32)
GLA_HEADS_PER_STEP = 2


def _gla_constants():
    c = HGRN_CHUNK
    i = np.arange(c)[:, None]
    t = np.arange(c)[None, :]
    flip = lambda m: m[::-1, ::-1]
    mats = [t <= i]
    masks = []
    for s in _GLA_LEVELS:
        blk = i // s
        right = blk % 2 == 1
        mats.append(right & (t > blk * s) & (t <= i))
        mats.append(~right & (t > i) & (t <= (blk + 1) * s))
        masks.append(right & (t // s == blk - 1))
    mats.append(t > i)
    g_f = np.concatenate(mats, 0).astype(np.float32)
    g_b = np.concatenate([flip(m) for m in mats], 0).astype(np.float32)
    m_f = np.stack(masks).astype(np.float32)
    m_b = np.stack([flip(m) for m in masks]).astype(np.float32)
    tile3 = lambda g: jnp.asarray(np.tile(g, (1, 3)), BF16)
    return tile3(g_f), tile3(g_b), jnp.asarray(m_f), jnp.asarray(m_b)


def _nt_dot(a, b):
    return lax.dot_general(a, b, (((1,), (1,)), ((), ())), preferred_element_type=F32)


def _gla_chunk(a, q, v, lb_floor, one_minus_lb, gmat, masks, st, reverse):
    c = HGRN_CHUNK
    nsub = c // 8
    g_rows = (2 + 2 * len(_GLA_LEVELS)) * c
    f = lb_floor + one_minus_lb * jax.nn.sigmoid(a)
    lf = jnp.log(f)
    k = 1.0 - f
    hi = lf.astype(BF16)
    r1 = lf - hi.astype(F32)
    mid = r1.astype(BF16)
    lo = (r1 - mid.astype(F32)).astype(BF16)
    ex = jnp.dot(gmat, jnp.concatenate([hi, mid, lo], axis=0), preferred_element_type=F32)
    cum = ex[0:c]
    vb16 = v.astype(BF16)

    p = jnp.zeros((c, c), F32)
    for li in range(len(_GLA_LEVELS)):
        ea = ex[c * (1 + 2 * li):c * (2 + 2 * li)]
        eb = ex[c * (2 + 2 * li):c * (3 + 2 * li)]
        qa = (q * jnp.exp(jnp.minimum(ea, 0.0))).astype(BF16)
        kb = (k * jnp.exp(jnp.minimum(eb, 0.0))).astype(BF16)
        p = p + masks[li] * _nt_dot(qa, kb)
    o = jnp.dot(p.astype(BF16), vb16, preferred_element_type=F32)

    q3 = q.reshape(nsub, 8, LANES)
    k3 = k.reshape(nsub, 8, LANES)
    c3 = cum.reshape(nsub, 8, LANES)
    v3 = v.reshape(nsub, 8, LANES)
    ii = lax.broadcasted_iota(jnp.int32, (nsub, 8, LANES), 1)
    ws = []
    for jj in range(8):
        kj = jnp.broadcast_to(k3[:, jj:jj + 1, :], (nsub, 8, LANES))
        cj = jnp.broadcast_to(c3[:, jj:jj + 1, :], (nsub, 8, LANES))
        w = q3 * kj * jnp.exp(jnp.minimum(c3 - cj, 0.0))
        keep = (ii <= jj) if reverse else (ii >= jj)
        ws.append(jnp.where(keep, w, 0.0).reshape(c, LANES).astype(BF16))
    rs = jnp.dot(jnp.concatenate(ws, axis=0), jnp.ones((LANES, LANES), BF16), preferred_element_type=F32)
    o3 = o.reshape(nsub, 8, LANES)
    for jj in range(8):
        vj = jnp.broadcast_to(v3[:, jj:jj + 1, :], (nsub, 8, LANES))
        o3 = o3 + rs[jj * c:(jj + 1) * c].reshape(nsub, 8, LANES) * vj
    o = o3.reshape(c, LANES)

    o = o + _nt_dot((q * jnp.exp(cum)).astype(BF16), st.astype(BF16))
    ke = (k * jnp.exp(ex[g_rows - c:g_rows])).astype(BF16)
    total = cum[0:1] if reverse else cum[c - 1:c]
    st_new = jnp.exp(total) * st + jnp.dot(v.T.astype(BF16), ke, preferred_element_type=F32)
    return o, st_new


def _gla_kernel(af_ref, ab_ref, v_ref, q_ref, g_ref, lbl_ref, ng_ref, gf_ref, gb_ref, mf_ref, mb_ref,
                o_ref, of_ref, ob_ref, sf_ref, sb_ref, *, layer, t_ctx, t_lat):
    c = HGRN_CHUNK
    logits = lbl_ref[...]
    e = jnp.exp(logits - jnp.max(logits, axis=0, keepdims=True))
    pr = e / jnp.sum(e, axis=0, keepdims=True)
    csum = pr[0]
    for dd in range(1, layer + 1):
        csum = csum + pr[dd]
    lb = csum - pr[0]
    lb_floor = jnp.maximum(lb, LB_FLOOR)
    one_minus_lb = 1.0 - lb
    masks_f = [mf_ref[li] for li in range(len(_GLA_LEVELS))]
    masks_b = [mb_ref[li] for li in range(len(_GLA_LEVELS))]
    gmat_f = gf_ref[...]
    gmat_b = gb_ref[...]

    sf_ref[...] = jnp.zeros_like(sf_ref)
    sb_ref[...] = jnp.zeros_like(sb_ref)

    def scan(row0, nchunks):
        def body(ci, carry):
            rf = pl.multiple_of(row0 + ci * c, c)
            rb = pl.multiple_of(row0 + (nchunks - 1 - ci) * c, c)
            lanes = [slice(hh * LANES, (hh + 1) * LANES) for hh in range(GLA_HEADS_PER_STEP)]
            ins_f = [(af_ref[pl.ds(rf, c), ln], q_ref[pl.ds(rf, c), ln], v_ref[pl.ds(rf, c), ln]) for ln in lanes]
            ins_b = [(ab_ref[pl.ds(rb, c), ln], q_ref[pl.ds(rb, c), ln], v_ref[pl.ds(rb, c), ln]) for ln in lanes]
            st_f = [sf_ref[hh] for hh in range(GLA_HEADS_PER_STEP)]
            st_b = [sb_ref[hh] for hh in range(GLA_HEADS_PER_STEP)]
            outs = []
            for hh, ln in enumerate(lanes):
                outs.append(_gla_chunk(*ins_f[hh], lb_floor[0:1, ln], one_minus_lb[0:1, ln], gmat_f, masks_f,
                                       st_f[hh], False))
                outs.append(_gla_chunk(*ins_b[hh], lb_floor[1:2, ln], one_minus_lb[1:2, ln], gmat_b, masks_b,
                                       st_b[hh], True))
            for hh, ln in enumerate(lanes):
                (o_f, s_f), (o_b, s_b) = outs[2 * hh], outs[2 * hh + 1]
                of_ref[pl.ds(rf, c), ln] = o_f
                ob_ref[pl.ds(rb, c), ln] = o_b
                sf_ref[hh] = s_f
                sb_ref[hh] = s_b
            return carry
        lax.fori_loop(0, nchunks, body, 0, unroll=4)

    scan(0, t_ctx // c)
    scan(t_ctx, t_lat // c)

    for hh in range(GLA_HEADS_PER_STEP):
        ln = slice(hh * LANES, (hh + 1) * LANES)
        o = of_ref[:, ln] + ob_ref[:, ln]
        y = _rms(o) * ng_ref[...] * jax.nn.sigmoid(g_ref[:, ln])
        o_ref[:, ln] = y.astype(o_ref.dtype)


def hgrn_branch(d, z, lb_logits, norm_g, layer, consts):
    g_f, g_b, m_f, m_b = consts
    w = GLA_HEADS_PER_STEP * LANES
    col = lambda base: (lambda b, h: (b, base // w + h))
    zspec = lambda base: pl.BlockSpec((d.pb, w), col(base))
    const2 = lambda shape: pl.BlockSpec(shape, lambda b, h: (0,) * len(shape))
    state = pltpu.VMEM((GLA_HEADS_PER_STEP, HGRN_DV, HGRN_DK), F32)
    return pl.pallas_call(
        functools.partial(_gla_kernel, layer=layer, t_ctx=d.t_ctx, t_lat=d.t_lat),
        grid=(d.nb, HGRN_HEADS // GLA_HEADS_PER_STEP),
        in_specs=[zspec(ZF), zspec(ZB), zspec(ZI), zspec(ZHQ), zspec(ZHG),
                  pl.BlockSpec((DEPTH, 2, w), lambda b, h: (0, 0, h)),
                  const2((1, HGRN_DV)), const2(g_f.shape), const2(g_b.shape), const2(m_f.shape), const2(m_b.shape)],
        out_specs=pl.BlockSpec((d.pb, w), lambda b, h: (b, h)),
        out_shape=jax.ShapeDtypeStruct((d.m_all, D_MODEL), BF16),
        scratch_shapes=[pltpu.VMEM((d.pb, w), F32), pltpu.VMEM((d.pb, w), F32), state, state],
        compiler_params=_params(("arbitrary", "arbitrary")),
        name="hgrn_branch",
    )(z, z, z, z, z, lb_logits, norm_g.reshape(1, HGRN_DV), g_f, g_b, m_f, m_b)


def _rope_tables(d):
    half = ROPE_AXIS_DIM // 2
    tok = np.arange(d.t_lat)
    inv = ROPE_THETA ** (-np.arange(0, ROPE_AXIS_DIM, 2, dtype=np.float32) / ROPE_AXIS_DIM)
    ang_r = (tok // GRID_W).astype(np.float32)[:, None] * inv.astype(np.float32)
    ang_c = (tok % GRID_W).astype(np.float32)[:, None] * inv.astype(np.float32)
    ang = jnp.asarray(np.concatenate([ang_r, ang_r, ang_c, ang_c], axis=1), F32)
    first = jnp.asarray((np.arange(HEAD_DIM) % ROPE_AXIS_DIM) < half)
    cos = jnp.cos(ang)
    sin = jnp.sin(ang)
    sin_a = jnp.where(first, -sin, 0.0)
    sin_b = jnp.where(first, 0.0, sin)
    pad = lambda t, fill: jnp.concatenate([jnp.full((d.t_ctx, HEAD_DIM), fill, F32), t], axis=0)
    return pad(cos, 1.0), pad(sin_a, 0.0), pad(sin_b, 0.0)


def _rope(u, cos, sin_a, sin_b):
    half = ROPE_AXIS_DIM // 2
    return u * cos + pltpu.roll(u, HEAD_DIM - half, 1) * sin_a + pltpu.roll(u, half, 1) * sin_b


def _attn_kernel(q_ref, k_ref, v_ref, g_ref, ck_ref, sak_ref, sbk_ref, cq_ref, saq_ref, sbq_ref, br_ref,
                 o_ref, kbuf, vbuf, *, t_ctx):
    del br_ref
    qi = pl.program_id(2)
    tq = q_ref.shape[0]
    scale = HEAD_DIM ** -0.5

    @pl.when(qi == 0)
    def _():
        kn = _rms(k_ref[...]) * g_ref[1:2, :]
        kbuf[...] = _rope(kn, ck_ref[...], sak_ref[...], sbk_ref[...]).astype(BF16)
        vbuf[...] = v_ref[...].astype(BF16)

    q = q_ref[...]
    cq, saq, sbq = cq_ref[...], saq_ref[...], sbq_ref[...]
    qs = []
    for gi in range(ATTN_GROUP):
        qn = _rms(q[:, gi * HEAD_DIM:(gi + 1) * HEAD_DIM]) * g_ref[0:1, :]
        qs.append(_rope(qn, cq, saq, sbq).astype(BF16))
    q4 = jnp.concatenate(qs, axis=0)

    def attend(keys, vals):
        s = _nt_dot(q4, keys)
        m = jnp.max(s, axis=-1, keepdims=True)
        p = jnp.exp2((s - m) * (scale * LOG2_E))
        l = jnp.sum(p, axis=-1, keepdims=True)
        o = jnp.dot(p.astype(BF16), vals, preferred_element_type=F32) / l
        for gi in range(ATTN_GROUP):
            o_ref[:, gi * HEAD_DIM:(gi + 1) * HEAD_DIM] = o[gi * tq:(gi + 1) * tq].astype(o_ref.dtype)

    @pl.when(qi == 0)
    def _():
        attend(kbuf[0:t_ctx, :], vbuf[0:t_ctx, :])

    @pl.when(qi > 0)
    def _():
        attend(kbuf[...], vbuf[...])


def attention_branch(d, z, br, qk_norm_g, tables):
    cos, sin_a, sin_b = tables
    tq = d.t_ctx
    nq = d.tiles_per_batch
    gw = ATTN_GROUP * HEAD_DIM
    kspec = lambda base: pl.BlockSpec((d.pb, HEAD_DIM), lambda b, h, qi: (b, base // HEAD_DIM + h))
    tab_k = pl.BlockSpec((d.pb, HEAD_DIM), lambda b, h, qi: (0, 0))
    tab_q = pl.BlockSpec((tq, HEAD_DIM), lambda b, h, qi: (qi, 0))
    return pl.pallas_call(
        functools.partial(_attn_kernel, t_ctx=d.t_ctx),
        grid=(d.nb, ATTN_KV_HEADS, nq),
        in_specs=[pl.BlockSpec((tq, gw), lambda b, h, qi: (b * nq + qi, ZAQ // gw + h)),
                  kspec(ZK), kspec(ZV),
                  pl.BlockSpec((2, HEAD_DIM), lambda b, h, qi: (0, 0)),
                  tab_k, tab_k, tab_k, tab_q, tab_q, tab_q,
                  pl.BlockSpec(memory_space=pl.ANY)],
        out_specs=pl.BlockSpec((tq, gw), lambda b, h, qi: (b * nq + qi, BRANCH_W // gw + h)),
        out_shape=jax.ShapeDtypeStruct((d.m_all, D_MODEL), BF16),
        scratch_shapes=[pltpu.VMEM((d.pb, HEAD_DIM), BF16), pltpu.VMEM((d.pb, HEAD_DIM), BF16)],
        input_output_aliases={10: 0},
        compiler_params=_params(("arbitrary", "arbitrary", "arbitrary")),
        name="attention_branch",
    )(z, z, z, qk_norm_g, cos, sin_a, sin_b, cos, sin_a, sin_b, br)


def _conv_kernel(sb_ref, sc_ref, su_ref, ga_ref, gg_ref,
                 scp_ref, sup_ref, gap_ref, ggp_ref, scn_ref, sun_ref, gan_ref, ggn_ref,
                 ws_ref, wc_ref, bc_ref, lg_ref, lb_ref, br_ref, o_ref, ps_ref, pc_ref, *, tiles_per_batch):
    del br_ref
    tt = sb_ref.shape[0]
    within = pl.program_id(0) % tiles_per_batch
    prev_ok = (within >= 2).astype(F32)
    next_ok = jnp.logical_and(within >= 1, within <= tiles_per_batch - 2).astype(F32)

    glu = lambda a, g: a * jax.nn.sigmoid(g)
    ps_ref[0:HALO, :] = scp_ref[...] * sup_ref[...] * prev_ok
    ps_ref[HALO:HALO + tt, :] = sc_ref[...] * su_ref[...]
    ps_ref[HALO + tt:2 * HALO + tt, :] = scn_ref[...] * sun_ref[...] * next_ok
    pc_ref[0:HALO, :] = glu(gap_ref[...], ggp_ref[...]) * prev_ok
    pc_ref[HALO:HALO + tt, :] = glu(ga_ref[...], gg_ref[...])
    pc_ref[HALO + tt:2 * HALO + tt, :] = glu(gan_ref[...], ggn_ref[...]) * next_ok

    acc = jnp.zeros((tt, BRANCH_W), F32)
    for tau in range(SHORT_CONV_W):
        acc = acc + ps_ref[pl.ds(HALO - SHORT_CONV_W // 2 + tau, tt), :] * ws_ref[tau:tau + 1, :]
    o_ref[:, 0:BRANCH_W] = (sb_ref[...] * acc).astype(o_ref.dtype)

    acc = jnp.zeros((tt, BRANCH_W), F32)
    for tau in range(CONF_CONV_W):
        acc = acc + pc_ref[pl.ds(HALO - CONF_CONV_W // 2 + tau, tt), :] * wc_ref[tau:tau + 1, :]
    u = acc + bc_ref[...]
    uc = u - jnp.mean(u, axis=-1, keepdims=True)
    y = uc * lax.rsqrt(jnp.mean(uc * uc, axis=-1, keepdims=True) + EPS) * lg_ref[...] + lb_ref[...]
    o_ref[:, BRANCH_W:2 * BRANCH_W] = (y * jax.nn.sigmoid(y)).astype(o_ref.dtype)


def conv_branches(d, z, br, short_w, dw_w, dw_b, ln_g, ln_b):
    tt = d.t_ctx
    per = tt // HALO
    last = d.m_all // HALO - 1
    cur = lambda base: pl.BlockSpec((tt, BRANCH_W), lambda i: (i, base // BRANCH_W))
    prv = lambda base: pl.BlockSpec((HALO, BRANCH_W), lambda i: (jnp.maximum(i * per - 1, 0), base // BRANCH_W))
    nxt = lambda base: pl.BlockSpec((HALO, BRANCH_W), lambda i: (jnp.minimum((i + 1) * per, last), base // BRANCH_W))
    full = lambda shape: pl.BlockSpec(shape, lambda i: (0,) * len(shape))
    row = full((1, BRANCH_W))
    return pl.pallas_call(
        functools.partial(_conv_kernel, tiles_per_batch=d.tiles_per_batch),
        grid=(d.m_all // tt,),
        in_specs=[cur(ZSB), cur(ZSC), cur(ZSU), cur(ZGA), cur(ZGG),
                  prv(ZSC), prv(ZSU), prv(ZGA), prv(ZGG), nxt(ZSC), nxt(ZSU), nxt(ZGA), nxt(ZGG),
                  full((SHORT_CONV_W, BRANCH_W)), full((CONF_CONV_W, BRANCH_W)), row, row, row,
                  pl.BlockSpec(memory_space=pl.ANY)],
        out_specs=pl.BlockSpec((tt, 2 * BRANCH_W), lambda i: (i, 1)),
        out_shape=jax.ShapeDtypeStruct((d.m_all, D_MODEL), BF16),
        scratch_shapes=[pltpu.VMEM((tt + 2 * HALO, BRANCH_W), F32), pltpu.VMEM((tt + 2 * HALO, BRANCH_W), F32)],
        input_output_aliases={18: 0},
        compiler_params=_params(("arbitrary",)),
        name="conv_branches",
    )(z, z, z, z, z, z, z, z, z, z, z, z, z, short_w, dw_w, dw_b.reshape(1, BRANCH_W),
      ln_g.reshape(1, BRANCH_W), ln_b.reshape(1, BRANCH_W), br)


Z_TN = 2 * AKV
assert OFF_K % Z_TN == 0 and OFF_HQ - OFF_K == Z_TN and Z_COLS % Z_TN == 0


def _z_col_block(j):
    kv = OFF_K // Z_TN
    return jnp.where(j < kv, j, jnp.where(j == kv, Z_COLS // Z_TN - 1, j - 1))


def kernel(x, c, ctx, c_ctx, w_mod, b_mod, norm_g, ffn_w1, ffn_w2, w_in, hgrn_lb_logits, hgrn_norm_g, qk_norm_g,
           short_conv_w, conf_dw_w, conf_dw_b, conf_ln_g, conf_ln_b, w_branch, w_out):
    d = Dims(nb=x.shape[0], t_lat=x.shape[1], t_ctx=ctx.shape[1])
    assert d.nb + 1 <= MOD_ROWS and d.t_lat % d.t_ctx == 0 and d.t_ctx % HGRN_CHUNK == 0 and d.t_ctx % HALO == 0
    assert d.t_lat % GRID_W == 0 and d.t_ctx >= HALO

    cc = jnp.concatenate([c, c_ctx[None], jnp.zeros((MOD_ROWS - d.nb - 1, D_MODEL), F32)], axis=0)
    modv_all = mod_vectors(cc, w_mod, b_mod).reshape(DEPTH, MOD_ROWS, 1, N_MOD * D_MODEL)

    s = jnp.concatenate([ctx, x], axis=1).reshape(d.m_all, D_MODEL)
    w1_bf = ffn_w1.astype(BF16)
    w2_bf = ffn_w2.astype(BF16)
    w_in_bf = w_in.astype(BF16)
    w_branch_bf = w_branch.astype(BF16)
    w_out_bf = w_out.astype(BF16)
    gla_consts = _gla_constants()
    tables = _rope_tables(d)

    g6_all = norm_g.reshape(DEPTH, 6, 1, D_MODEL)
    h = normmod(d, s, g6_all[0], modv_all[0], 0, 0)
    for l in range(DEPTH):
        last = l == DEPTH - 1
        modv = modv_all[l]
        g6 = g6_all[l]

        y = matmul(swiglu_up(h, w1_bf, (l, 0)), w2_bf, (l, 0), D_MODEL, 1024, BF16)
        s, h = resid_normmod(d, s, y, g6, modv, 1, 2, FFN_RESIDUAL, 2, 3)

        z = matmul(h, w_in_bf, (l,), Z_COLS, Z_TN, F32, out_col_block=_z_col_block)
        br = hgrn_branch(d, z, hgrn_lb_logits, hgrn_norm_g[l], l, gla_consts)
        br = attention_branch(d, z, br, qk_norm_g[l], tables)
        br = conv_branches(d, z, br, short_conv_w[l], conf_dw_w[l], conf_dw_b[l], conf_ln_g[l], conf_ln_b[l])
        y = matmul(gated_merge(h, br, w_in_bf, w_branch_bf, l), w_out_bf, (l,), D_MODEL, 1024, BF16)

        if last:
            s, h = resid_normmod_latent(d, s, y, g6, modv, 3, 5, 1.0, 4, 6)
        else:
            s, h = resid_normmod(d, s, y, g6, modv, 3, 5, 1.0, 4, 6)
        y = matmul(swiglu_up(h, w1_bf, (l, 1)), w2_bf, (l, 1), D_MODEL, 1024, BF16)
        if last:
            s = resid_last_latent(d, s, y, g6, modv, 5, 8, FFN_RESIDUAL)
        else:
            s, h = resid_normmod(d, s, y, g6, modv, 5, 8, FFN_RESIDUAL, 0, 0, g6_all[l + 1], modv_all[l + 1])
    return s.reshape(d.nb, d.t_lat, D_MODEL)
```

```python
import functools
from typing import NamedTuple

import numpy as np
import jax
import jax.numpy as jnp
from jax import lax
from jax.experimental import pallas as pl
from jax.experimental.pallas import tpu as pltpu

D_MODEL = 4096
DEPTH = 2
GRID_W = 64
N_BRANCH = 4
BRANCH_W = D_MODEL // 4
HGRN_DK = 128
HGRN_DV = 128
HGRN_HEADS = BRANCH_W // HGRN_DV
HGRN_CHUNK = 64
HEAD_DIM = 128
ATTN_Q_HEADS = BRANCH_W // HEAD_DIM
ATTN_KV_HEADS = ATTN_Q_HEADS // 4
ATTN_GROUP = ATTN_Q_HEADS // ATTN_KV_HEADS
ROPE_AXIS_DIM = HEAD_DIM // 2
ROPE_THETA = 10000.0
SHORT_CONV_W = 3
CONF_CONV_W = 31
FFN_DIM = D_MODEL
FFN_RESIDUAL = 0.5
N_MOD = 9
EPS = 1e-6
LB_FLOOR = 1e-30
LOG2_E = 1.4426950408889634

HK = HGRN_HEADS * HGRN_DK
HV = HGRN_HEADS * HGRN_DV
AQ = ATTN_Q_HEADS * HEAD_DIM
AKV = ATTN_KV_HEADS * HEAD_DIM
OFF_K = 2 * HK + HV
OFF_HQ = OFF_K + 2 * AKV
OFF_GATE = OFF_HQ + HK + HV + AQ + 3 * BRANCH_W + 2 * BRANCH_W
N_IN_COLS = OFF_GATE + N_BRANCH * D_MODEL

ZF, ZB, ZI, ZHQ, ZHG, ZAQ, ZSB, ZSC, ZSU, ZGA, ZGG = (k * BRANCH_W for k in range(11))
ZK = 11 * BRANCH_W
ZV = ZK + AKV
Z_COLS = ZV + AKV
assert Z_COLS == OFF_GATE

MOD_ROWS = 16
HALO = 16
assert CONF_CONV_W // 2 < HALO and HALO % 8 == 0
LANES = 128
MXU_WIDTH = 256

V7X_VMEM_BYTES = 64 * 1024 * 1024
VMEM_LIMIT = V7X_VMEM_BYTES - 8 * 1024 * 1024

BF16 = jnp.bfloat16
F32 = jnp.float32


class Dims(NamedTuple):
    nb: int
    t_lat: int
    t_ctx: int

    @property
    def pb(self):
        return self.t_lat + self.t_ctx

    @property
    def m_all(self):
        return self.nb * self.pb

    @property
    def tiles_per_batch(self):
        return self.pb // self.t_ctx


def _params(sem):
    return pltpu.CompilerParams(dimension_semantics=sem, vmem_limit_bytes=VMEM_LIMIT)


def _mod_kernel(cc_ref, w_ref, b_ref, o_ref):
    a = cc_ref[...]
    a = (a * jax.nn.sigmoid(a)).astype(BF16)
    o_ref[...] = jnp.dot(a, w_ref[...].astype(BF16), preferred_element_type=F32) + b_ref[...]


def mod_vectors(cc, w_mod, b_mod):
    tn = 512
    n = N_MOD * D_MODEL
    return pl.pallas_call(
        _mod_kernel,
        grid=(DEPTH, n // tn),
        in_specs=[pl.BlockSpec((MOD_ROWS, D_MODEL), lambda l, j: (0, 0)),
                  pl.BlockSpec((None, D_MODEL, tn), lambda l, j: (l, 0, j)),
                  pl.BlockSpec((None, 1, tn), lambda l, j: (l, 0, j))],
        out_specs=pl.BlockSpec((None, MOD_ROWS, tn), lambda l, j: (l, 0, j)),
        out_shape=jax.ShapeDtypeStruct((DEPTH, MOD_ROWS, n), F32),
        compiler_params=_params(("arbitrary", "arbitrary")),
        name="mod_vectors",
    )(cc, w_mod, b_mod.reshape(DEPTH, 1, n))


def _mod_spec(d, chunk):
    tpb = d.tiles_per_batch
    return pl.BlockSpec((None, 1, D_MODEL), lambda i: (jnp.where(i % tpb == 0, d.nb, i // tpb), 0, chunk))


def _g_spec(k):
    return pl.BlockSpec((None, 1, D_MODEL), lambda i: (k, 0, 0))


def _row_spec(d):
    return pl.BlockSpec((d.t_ctx, D_MODEL), lambda i: (i, 0))


def _lat_in_spec(d, chunk=None):
    tpb = d.tiles_per_batch
    if chunk is None:
        return pl.BlockSpec((d.t_ctx, D_MODEL), lambda b, j: (b * tpb + 1 + j, 0))
    return pl.BlockSpec((None, 1, D_MODEL), lambda b, j: (b, 0, chunk))


def _rms(x):
    return x * lax.rsqrt(jnp.mean(x * x, axis=-1, keepdims=True) + EPS)


def _normmod_kernel(s_ref, g_ref, sh_ref, sc_ref, o_ref):
    y = _rms(s_ref[...]) * g_ref[...]
    o_ref[...] = (y * (1 + sc_ref[...]) + sh_ref[...]).astype(o_ref.dtype)


def normmod(d, s, g6, modv, g_idx, shift_idx):
    return pl.pallas_call(
        _normmod_kernel,
        grid=(d.m_all // d.t_ctx,),
        in_specs=[_row_spec(d), _g_spec(g_idx), _mod_spec(d, shift_idx), _mod_spec(d, shift_idx + 1)],
        out_specs=_row_spec(d),
        out_shape=jax.ShapeDtypeStruct((d.m_all, D_MODEL), BF16),
        compiler_params=_params(("arbitrary",)),
        name="normmod",
    )(s, g6, modv, modv)


def _resid_kernel(s_ref, y_ref, gpost_ref, gate_ref, gpre_ref, sh_ref, sc_ref, so_ref, ho_ref, *, coef):
    yn = _rms(y_ref[...].astype(F32)) * gpost_ref[...]
    s = s_ref[...] + (coef * gate_ref[...]) * yn
    so_ref[...] = s
    h = _rms(s) * gpre_ref[...]
    ho_ref[...] = (h * (1 + sc_ref[...]) + sh_ref[...]).astype(ho_ref.dtype)


def _resid_last_kernel(s_ref, y_ref, gpost_ref, gate_ref, so_ref, *, coef):
    yn = _rms(y_ref[...].astype(F32)) * gpost_ref[...]
    so_ref[...] = s_ref[...] + (coef * gate_ref[...]) * yn


def resid_normmod(d, s, y, g6, modv, gpost_idx, gate_idx, coef, gpre_idx, shift_idx, g6_pre=None, modv_pre=None):
    g6_pre = g6 if g6_pre is None else g6_pre
    modv_pre = modv if modv_pre is None else modv_pre
    return pl.pallas_call(
        functools.partial(_resid_kernel, coef=coef),
        grid=(d.m_all // d.t_ctx,),
        in_specs=[_row_spec(d), _row_spec(d), _g_spec(gpost_idx), _mod_spec(d, gate_idx),
                  _g_spec(gpre_idx), _mod_spec(d, shift_idx), _mod_spec(d, shift_idx + 1)],
        out_specs=[_row_spec(d), _row_spec(d)],
        out_shape=[jax.ShapeDtypeStruct((d.m_all, D_MODEL), F32), jax.ShapeDtypeStruct((d.m_all, D_MODEL), BF16)],
        compiler_params=_params(("arbitrary",)),
        name="resid_normmod",
    )(s, y, g6, modv, g6_pre, modv_pre, modv_pre)


def resid_normmod_latent(d, s, y, g6, modv, gpost_idx, gate_idx, coef, gpre_idx, shift_idx):
    nl = d.t_lat // d.t_ctx
    g_spec = lambda k: pl.BlockSpec((None, 1, D_MODEL), lambda b, j: (k, 0, 0))
    dense = pl.BlockSpec((d.t_ctx, D_MODEL), lambda b, j: (b * nl + j, 0))
    return pl.pallas_call(
        functools.partial(_resid_kernel, coef=coef),
        grid=(d.nb, nl),
        in_specs=[_lat_in_spec(d), _lat_in_spec(d), g_spec(gpost_idx), _lat_in_spec(d, gate_idx),
                  g_spec(gpre_idx), _lat_in_spec(d, shift_idx), _lat_in_spec(d, shift_idx + 1)],
        out_specs=[dense, dense],
        out_shape=[jax.ShapeDtypeStruct((d.nb * d.t_lat, D_MODEL), F32),
                   jax.ShapeDtypeStruct((d.nb * d.t_lat, D_MODEL), BF16)],
        compiler_params=_params(("arbitrary", "arbitrary")),
        name="resid_normmod_latent",
    )(s, y, g6, modv, g6, modv, modv)


def resid_last_latent(d, s, y, g6, modv, gpost_idx, gate_idx, coef):
    nl = d.t_lat // d.t_ctx
    dense = pl.BlockSpec((d.t_ctx, D_MODEL), lambda b, j: (b * nl + j, 0))
    return pl.pallas_call(
        functools.partial(_resid_last_kernel, coef=coef),
        grid=(d.nb, nl),
        in_specs=[dense, dense, pl.BlockSpec((None, 1, D_MODEL), lambda b, j: (gpost_idx, 0, 0)),
                  pl.BlockSpec((None, 1, D_MODEL), lambda b, j: (b, 0, gate_idx))],
        out_specs=dense,
        out_shape=jax.ShapeDtypeStruct((d.nb * d.t_lat, D_MODEL), F32),
        compiler_params=_params(("arbitrary", "arbitrary")),
        name="resid_last_latent",
    )(s, y, g6, modv)


def _row_tile(m):
    return next(t for t in (1024, 512, 256, 128) if m % t == 0)


def _mm_kernel(x_ref, w_ref, o_ref):
    o_ref[...] = jnp.dot(x_ref[...], w_ref[...], preferred_element_type=F32).astype(o_ref.dtype)


def _w_spec(lead, k, tn, col):
    return pl.BlockSpec((None,) * len(lead) + (k, tn), lambda *g: tuple(lead) + (0, col(*g)))


def matmul(x, w, lead, ncols, tn, out_dtype, out_col_block=lambda j: j):
    m, k = x.shape
    tm = _row_tile(m)
    return pl.pallas_call(
        _mm_kernel,
        grid=(m // tm, ncols // tn),
        in_specs=[pl.BlockSpec((tm, k), lambda i, j: (i, 0)),
                  _w_spec(lead, k, tn, lambda i, j: j)],
        out_specs=pl.BlockSpec((tm, tn), lambda i, j: (i, out_col_block(j))),
        out_shape=jax.ShapeDtypeStruct((m, ncols), out_dtype),
        compiler_params=_params(("arbitrary", "arbitrary")),
        name="matmul",
    )(x, w)


def _col_halves(n):
    half = n // 2
    return (slice(0, half), slice(half, n)) if half % MXU_WIDTH == 0 else (slice(0, n),)


def _swiglu_kernel(x_ref, wa_ref, wb_ref, o_ref):
    for cs in _col_halves(o_ref.shape[1]):
        a = jnp.dot(x_ref[...], wa_ref[:, cs], preferred_element_type=F32)
        b = jnp.dot(x_ref[...], wb_ref[:, cs], preferred_element_type=F32)
        o_ref[:, cs] = (a * jax.nn.sigmoid(a) * b).astype(o_ref.dtype)


def swiglu_up(h, w1, lead):
    m = h.shape[0]
    tm = _row_tile(m)
    tn = 512
    nb = FFN_DIM // tn
    return pl.pallas_call(
        _swiglu_kernel,
        grid=(m // tm, nb),
        in_specs=[pl.BlockSpec((tm, D_MODEL), lambda i, j: (i, 0)),
                  _w_spec(lead, D_MODEL, tn, lambda i, j: j),
                  _w_spec(lead, D_MODEL, tn, lambda i, j: j + nb)],
        out_specs=pl.BlockSpec((tm, tn), lambda i, j: (i, j)),
        out_shape=jax.ShapeDtypeStruct((m, FFN_DIM), BF16),
        compiler_params=_params(("arbitrary", "arbitrary")),
        name="swiglu_up",
    )(h, w1, w1)


def _merge_kernel(h_ref, br_ref, wg_ref, wb_ref, o_ref, acc_ref):
    b = pl.program_id(2)

    @pl.when(b == 0)
    def _():
        acc_ref[...] = jnp.zeros_like(acc_ref)

    for cs in _col_halves(o_ref.shape[1]):
        g = jnp.dot(h_ref[...], wg_ref[:, cs], preferred_element_type=F32)
        p = jnp.dot(br_ref[...], wb_ref[:, cs], preferred_element_type=F32)
        acc_ref[:, cs] += jax.nn.sigmoid(g) * p

    @pl.when(b == N_BRANCH - 1)
    def _():
        o_ref[...] = acc_ref[...].astype(o_ref.dtype)


def gated_merge(h, br, w_in, w_branch, layer):
    m = h.shape[0]
    tm = _row_tile(m)
    tn = 512
    nb = D_MODEL // tn
    gate0 = OFF_GATE // tn
    return pl.pallas_call(
        _merge_kernel,
        grid=(m // tm, nb, N_BRANCH),
        in_specs=[pl.BlockSpec((tm, D_MODEL), lambda i, j, b: (i, 0)),
                  pl.BlockSpec((tm, BRANCH_W), lambda i, j, b: (i, b)),
                  _w_spec((layer,), D_MODEL, tn, lambda i, j, b: gate0 + b * nb + j),
                  pl.BlockSpec((None, None, BRANCH_W, tn), lambda i, j, b: (layer, b, 0, j))],
        out_specs=pl.BlockSpec((tm, tn), lambda i, j, b: (i, j)),
        out_shape=jax.ShapeDtypeStruct((m, D_MODEL), BF16),
        scratch_shapes=[pltpu.VMEM((tm, tn), F32)],
        compiler_params=_params(("arbitrary", "arbitrary", "arbitrary")),
        name="gated_merge",
    )(h, br, w_in, w_branch)


_GLA_LEVELS = (8, 16, 32)
GLA_HEADS_PER_STEP = 2


def _gla_constants():
    c = HGRN_CHUNK
    i = np.arange(c)[:, None]
    t = np.arange(c)[None, :]
    flip = lambda m: m[::-1, ::-1]
    mats = [t <= i]
    masks = []
    for s in _GLA_LEVELS:
        blk = i // s
        right = blk % 2 == 1
        mats.append(right & (t > blk * s) & (t <= i))
        mats.append(~right & (t > i) & (t <= (blk + 1) * s))
        masks.append(right & (t // s == blk - 1))
    mats.append(t > i)
    g_f = np.concatenate(mats, 0).astype(np.float32)
    g_b = np.concatenate([flip(m) for m in mats], 0).astype(np.float32)
    m_f = np.stack(masks).astype(np.float32)
    m_b = np.stack([flip(m) for m in masks]).astype(np.float32)
    tile3 = lambda g: jnp.asarray(np.tile(g, (1, 3)), BF16)
    return tile3(g_f), tile3(g_b), jnp.asarray(m_f), jnp.asarray(m_b)


def _nt_dot(a, b):
    return lax.dot_general(a, b, (((1,), (1,)), ((), ())), preferred_element_type=F32)


def _gla_chunk(a, q, v, lb_floor, one_minus_lb, gmat, masks, st, reverse):
    c = HGRN_CHUNK
    nsub = c // 8
    g_rows = (2 + 2 * len(_GLA_LEVELS)) * c
    f = lb_floor + one_minus_lb * jax.nn.sigmoid(a)
    lf = jnp.log(f)
    k = 1.0 - f
    hi = lf.astype(BF16)
    r1 = lf - hi.astype(F32)
    mid = r1.astype(BF16)
    lo = (r1 - mid.astype(F32)).astype(BF16)
    ex = jnp.dot(gmat, jnp.concatenate([hi, mid, lo], axis=0), preferred_element_type=F32)
    cum = ex[0:c]
    vb16 = v.astype(BF16)

    p = jnp.zeros((c, c), F32)
    for li in range(len(_GLA_LEVELS)):
        ea = ex[c * (1 + 2 * li):c * (2 + 2 * li)]
        eb = ex[c * (2 + 2 * li):c * (3 + 2 * li)]
        qa = (q * jnp.exp(jnp.minimum(ea, 0.0))).astype(BF16)
        kb = (k * jnp.exp(jnp.minimum(eb, 0.0))).astype(BF16)
        p = p + masks[li] * _nt_dot(qa, kb)
    o = jnp.dot(p.astype(BF16), vb16, preferred_element_type=F32)

    q3 = q.reshape(nsub, 8, LANES)
    k3 = k.reshape(nsub, 8, LANES)
    c3 = cum.reshape(nsub, 8, LANES)
    v3 = v.reshape(nsub, 8, LANES)
    ii = lax.broadcasted_iota(jnp.int32, (nsub, 8, LANES), 1)
    ws = []
    for jj in range(8):
        kj = jnp.broadcast_to(k3[:, jj:jj + 1, :], (nsub, 8, LANES))
        cj = jnp.broadcast_to(c3[:, jj:jj + 1, :], (nsub, 8, LANES))
        w = q3 * kj * jnp.exp(jnp.minimum(c3 - cj, 0.0))
        keep = (ii <= jj) if reverse else (ii >= jj)
        ws.append(jnp.where(keep, w, 0.0).reshape(c, LANES).astype(BF16))
    rs = jnp.dot(jnp.concatenate(ws, axis=0), jnp.ones((LANES, LANES), BF16), preferred_element_type=F32)
    o3 = o.reshape(nsub, 8, LANES)
    for jj in range(8):
        vj = jnp.broadcast_to(v3[:, jj:jj + 1, :], (nsub, 8, LANES))
        o3 = o3 + rs[jj * c:(jj + 1) * c].reshape(nsub, 8, LANES) * vj
    o = o3.reshape(c, LANES)

    o = o + _nt_dot((q * jnp.exp(cum)).astype(BF16), st.astype(BF16))
    ke = (k * jnp.exp(ex[g_rows - c:g_rows])).astype(BF16)
    total = cum[0:1] if reverse else cum[c - 1:c]
    st_new = jnp.exp(total) * st + jnp.dot(v.T.astype(BF16), ke, preferred_element_type=F32)
    return o, st_new


def _gla_kernel(af_ref, ab_ref, v_ref, q_ref, g_ref, lbl_ref, ng_ref, gf_ref, gb_ref, mf_ref, mb_ref,
                o_ref, of_ref, ob_ref, sf_ref, sb_ref, *, layer, t_ctx, t_lat):
    c = HGRN_CHUNK
    logits = lbl_ref[...]
    e = jnp.exp(logits - jnp.max(logits, axis=0, keepdims=True))
    pr = e / jnp.sum(e, axis=0, keepdims=True)
    csum = pr[0]
    for dd in range(1, layer + 1):
        csum = csum + pr[dd]
    lb = csum - pr[0]
    lb_floor = jnp.maximum(lb, LB_FLOOR)
    one_minus_lb = 1.0 - lb
    masks_f = [mf_ref[li] for li in range(len(_GLA_LEVELS))]
    masks_b = [mb_ref[li] for li in range(len(_GLA_LEVELS))]
    gmat_f = gf_ref[...]
    gmat_b = gb_ref[...]

    sf_ref[...] = jnp.zeros_like(sf_ref)
    sb_ref[...] = jnp.zeros_like(sb_ref)

    def scan(row0, nchunks):
        def body(ci, carry):
            rf = pl.multiple_of(row0 + ci * c, c)
            rb = pl.multiple_of(row0 + (nchunks - 1 - ci) * c, c)
            lanes = [slice(hh * LANES, (hh + 1) * LANES) for hh in range(GLA_HEADS_PER_STEP)]
            ins_f = [(af_ref[pl.ds(rf, c), ln], q_ref[pl.ds(rf, c), ln], v_ref[pl.ds(rf, c), ln]) for ln in lanes]
            ins_b = [(ab_ref[pl.ds(rb, c), ln], q_ref[pl.ds(rb, c), ln], v_ref[pl.ds(rb, c), ln]) for ln in lanes]
            st_f = [sf_ref[hh] for hh in range(GLA_HEADS_PER_STEP)]
            st_b = [sb_ref[hh] for hh in range(GLA_HEADS_PER_STEP)]
            outs = []
            for hh, ln in enumerate(lanes):
                outs.append(_gla_chunk(*ins_f[hh], lb_floor[0:1, ln], one_minus_lb[0:1, ln], gmat_f, masks_f,
                                       st_f[hh], False))
                outs.append(_gla_chunk(*ins_b[hh], lb_floor[1:2, ln], one_minus_lb[1:2, ln], gmat_b, masks_b,
                                       st_b[hh], True))
            for hh, ln in enumerate(lanes):
                (o_f, s_f), (o_b, s_b) = outs[2 * hh], outs[2 * hh + 1]
                of_ref[pl.ds(rf, c), ln] = o_f
                ob_ref[pl.ds(rb, c), ln] = o_b
                sf_ref[hh] = s_f
                sb_ref[hh] = s_b
            return carry
        lax.fori_loop(0, nchunks, body, 0, unroll=4)

    scan(0, t_ctx // c)
    scan(t_ctx, t_lat // c)

    for hh in range(GLA_HEADS_PER_STEP):
        ln = slice(hh * LANES, (hh + 1) * LANES)
        o = of_ref[:, ln] + ob_ref[:, ln]
        y = _rms(o) * ng_ref[...] * jax.nn.sigmoid(g_ref[:, ln])
        o_ref[:, ln] = y.astype(o_ref.dtype)


def hgrn_branch(d, z, lb_logits, norm_g, layer, consts):
    g_f, g_b, m_f, m_b = consts
    w = GLA_HEADS_PER_STEP * LANES
    col = lambda base: (lambda b, h: (b, base // w + h))
    zspec = lambda base: pl.BlockSpec((d.pb, w), col(base))
    const2 = lambda shape: pl.BlockSpec(shape, lambda b, h: (0,) * len(shape))
    state = pltpu.VMEM((GLA_HEADS_PER_STEP, HGRN_DV, HGRN_DK), F32)
    return pl.pallas_call(
        functools.partial(_gla_kernel, layer=layer, t_ctx=d.t_ctx, t_lat=d.t_lat),
        grid=(d.nb, HGRN_HEADS // GLA_HEADS_PER_STEP),
        in_specs=[zspec(ZF), zspec(ZB), zspec(ZI), zspec(ZHQ), zspec(ZHG),
                  pl.BlockSpec((DEPTH, 2, w), lambda b, h: (0, 0, h)),
                  const2((1, HGRN_DV)), const2(g_f.shape), const2(g_b.shape), const2(m_f.shape), const2(m_b.shape)],
        out_specs=pl.BlockSpec((d.pb, w), lambda b, h: (b, h)),
        out_shape=jax.ShapeDtypeStruct((d.m_all, D_MODEL), BF16),
        scratch_shapes=[pltpu.VMEM((d.pb, w), F32), pltpu.VMEM((d.pb, w), F32), state, state],
        compiler_params=_params(("arbitrary", "arbitrary")),
        name="hgrn_branch",
    )(z, z, z, z, z, lb_logits, norm_g.reshape(1, HGRN_DV), g_f, g_b, m_f, m_b)


def _rope_tables(d):
    half = ROPE_AXIS_DIM // 2
    tok = np.arange(d.t_lat)
    inv = ROPE_THETA ** (-np.arange(0, ROPE_AXIS_DIM, 2, dtype=np.float32) / ROPE_AXIS_DIM)
    ang_r = (tok // GRID_W).astype(np.float32)[:, None] * inv.astype(np.float32)
    ang_c = (tok % GRID_W).astype(np.float32)[:, None] * inv.astype(np.float32)
    ang = jnp.asarray(np.concatenate([ang_r, ang_r, ang_c, ang_c], axis=1), F32)
    first = jnp.asarray((np.arange(HEAD_DIM) % ROPE_AXIS_DIM) < half)
    cos = jnp.cos(ang)
    sin = jnp.sin(ang)
    sin_a = jnp.where(first, -sin, 0.0)
    sin_b = jnp.where(first, 0.0, sin)
    pad = lambda t, fill: jnp.concatenate([jnp.full((d.t_ctx, HEAD_DIM), fill, F32), t], axis=0)
    return pad(cos, 1.0), pad(sin_a, 0.0), pad(sin_b, 0.0)


def _rope(u, cos, sin_a, sin_b):
    half = ROPE_AXIS_DIM // 2
    return u * cos + pltpu.roll(u, HEAD_DIM - half, 1) * sin_a + pltpu.roll(u, half, 1) * sin_b


def _attn_kernel(q_ref, k_ref, v_ref, g_ref, ck_ref, sak_ref, sbk_ref, cq_ref, saq_ref, sbq_ref, br_ref,
                 o_ref, kbuf, vbuf, *, t_ctx):
    del br_ref
    qi = pl.program_id(2)
    tq = q_ref.shape[0]
    scale = HEAD_DIM ** -0.5

    @pl.when(qi == 0)
    def _():
        kn = _rms(k_ref[...]) * g_ref[1:2, :]
        kbuf[...] = _rope(kn, ck_ref[...], sak_ref[...], sbk_ref[...]).astype(BF16)
        vbuf[...] = v_ref[...].astype(BF16)

    q = q_ref[...]
    cq, saq, sbq = cq_ref[...], saq_ref[...], sbq_ref[...]
    qs = []
    for gi in range(ATTN_GROUP):
        qn = _rms(q[:, gi * HEAD_DIM:(gi + 1) * HEAD_DIM]) * g_ref[0:1, :]
        qs.append(_rope(qn, cq, saq, sbq).astype(BF16))
    q4 = jnp.concatenate(qs, axis=0)

    def attend(keys, vals):
        s = _nt_dot(q4, keys)
        m = jnp.max(s, axis=-1, keepdims=True)
        p = jnp.exp2((s - m) * (scale * LOG2_E))
        l = jnp.sum(p, axis=-1, keepdims=True)
        o = jnp.dot(p.astype(BF16), vals, preferred_element_type=F32) / l
        for gi in range(ATTN_GROUP):
            o_ref[:, gi * HEAD_DIM:(gi + 1) * HEAD_DIM] = o[gi * tq:(gi + 1) * tq].astype(o_ref.dtype)

    @pl.when(qi == 0)
    def _():
        attend(kbuf[0:t_ctx, :], vbuf[0:t_ctx, :])

    @pl.when(qi > 0)
    def _():
        attend(kbuf[...], vbuf[...])


def attention_branch(d, z, br, qk_norm_g, tables):
    cos, sin_a, sin_b = tables
    tq = d.t_ctx
    nq = d.tiles_per_batch
    gw = ATTN_GROUP * HEAD_DIM
    kspec = lambda base: pl.BlockSpec((d.pb, HEAD_DIM), lambda b, h, qi: (b, base // HEAD_DIM + h))
    tab_k = pl.BlockSpec((d.pb, HEAD_DIM), lambda b, h, qi: (0, 0))
    tab_q = pl.BlockSpec((tq, HEAD_DIM), lambda b, h, qi: (qi, 0))
    return pl.pallas_call(
        functools.partial(_attn_kernel, t_ctx=d.t_ctx),
        grid=(d.nb, ATTN_KV_HEADS, nq),
        in_specs=[pl.BlockSpec((tq, gw), lambda b, h, qi: (b * nq + qi, ZAQ // gw + h)),
                  kspec(ZK), kspec(ZV),
                  pl.BlockSpec((2, HEAD_DIM), lambda b, h, qi: (0, 0)),
                  tab_k, tab_k, tab_k, tab_q, tab_q, tab_q,
                  pl.BlockSpec(memory_space=pl.ANY)],
        out_specs=pl.BlockSpec((tq, gw), lambda b, h, qi: (b * nq + qi, BRANCH_W // gw + h)),
        out_shape=jax.ShapeDtypeStruct((d.m_all, D_MODEL), BF16),
        scratch_shapes=[pltpu.VMEM((d.pb, HEAD_DIM), BF16), pltpu.VMEM((d.pb, HEAD_DIM), BF16)],
        input_output_aliases={10: 0},
        compiler_params=_params(("arbitrary", "arbitrary", "arbitrary")),
        name="attention_branch",
    )(z, z, z, qk_norm_g, cos, sin_a, sin_b, cos, sin_a, sin_b, br)


def _conv_kernel(sb_ref, sc_ref, su_ref, ga_ref, gg_ref,
                 scp_ref, sup_ref, gap_ref, ggp_ref, scn_ref, sun_ref, gan_ref, ggn_ref,
                 ws_ref, wc_ref, bc_ref, lg_ref, lb_ref, br_ref, o_ref, ps_ref, pc_ref, pcs_ref, *, tiles_per_batch):
    del br_ref
    tt = sb_ref.shape[0]
    within = pl.program_id(0) % tiles_per_batch
    prev_ok = (within >= 2).astype(F32)
    next_ok = jnp.logical_and(within >= 1, within <= tiles_per_batch - 2).astype(F32)

    glu = lambda a, g: a * jax.nn.sigmoid(g)
    ps_ref[0:HALO, :] = scp_ref[...] * sup_ref[...] * prev_ok
    ps_ref[HALO:HALO + tt, :] = sc_ref[...] * su_ref[...]
    ps_ref[HALO + tt:2 * HALO + tt, :] = scn_ref[...] * sun_ref[...] * next_ok
    pc_ref[0:HALO, :] = glu(gap_ref[...], ggp_ref[...]) * prev_ok
    pc_ref[HALO:HALO + tt, :] = glu(ga_ref[...], gg_ref[...])
    pc_ref[HALO + tt:2 * HALO + tt, :] = glu(gan_ref[...], ggn_ref[...]) * next_ok

    acc = jnp.zeros((tt, BRANCH_W), F32)
    for tau in range(SHORT_CONV_W):
        acc = acc + ps_ref[pl.ds(HALO - SHORT_CONV_W // 2 + tau, tt), :] * ws_ref[tau:tau + 1, :]
    o_ref[:, 0:BRANCH_W] = (sb_ref[...] * acc).astype(o_ref.dtype)

    n = tt + 2 * HALO - 8
    for r in range(1, 8):
        pcs_ref[r - 1] = pc_ref[pl.ds(r, n), :]
    acc = jnp.zeros((tt, BRANCH_W), F32)
    for tau in range(CONF_CONV_W):
        off = HALO - CONF_CONV_W // 2 + tau
        base = (off // 8) * 8
        tap = pc_ref[pl.ds(base, tt), :] if off % 8 == 0 else pcs_ref[off % 8 - 1, pl.ds(base, tt), :]
        acc = acc + tap * wc_ref[tau:tau + 1, :]
    u = acc + bc_ref[...]
    uc = u - jnp.mean(u, axis=-1, keepdims=True)
    y = uc * lax.rsqrt(jnp.mean(uc * uc, axis=-1, keepdims=True) + EPS) * lg_ref[...] + lb_ref[...]
    o_ref[:, BRANCH_W:2 * BRANCH_W] = (y * jax.nn.sigmoid(y)).astype(o_ref.dtype)


def conv_branches(d, z, br, short_w, dw_w, dw_b, ln_g, ln_b):
    tt = d.t_ctx
    per = tt // HALO
    last = d.m_all // HALO - 1
    cur = lambda base: pl.BlockSpec((tt, BRANCH_W), lambda i: (i, base // BRANCH_W))
    prv = lambda base: pl.BlockSpec((HALO, BRANCH_W), lambda i: (jnp.maximum(i * per - 1, 0), base // BRANCH_W))
    nxt = lambda base: pl.BlockSpec((HALO, BRANCH_W), lambda i: (jnp.minimum((i + 1) * per, last), base // BRANCH_W))
    full = lambda shape: pl.BlockSpec(shape, lambda i: (0,) * len(shape))
    row = full((1, BRANCH_W))
    return pl.pallas_call(
        functools.partial(_conv_kernel, tiles_per_batch=d.tiles_per_batch),
        grid=(d.m_all // tt,),
        in_specs=[cur(ZSB), cur(ZSC), cur(ZSU), cur(ZGA), cur(ZGG),
                  prv(ZSC), prv(ZSU), prv(ZGA), prv(ZGG), nxt(ZSC), nxt(ZSU), nxt(ZGA), nxt(ZGG),
                  full((SHORT_CONV_W, BRANCH_W)), full((CONF_CONV_W, BRANCH_W)), row, row, row,
                  pl.BlockSpec(memory_space=pl.ANY)],
        out_specs=pl.BlockSpec((tt, 2 * BRANCH_W), lambda i: (i, 1)),
        out_shape=jax.ShapeDtypeStruct((d.m_all, D_MODEL), BF16),
        scratch_shapes=[pltpu.VMEM((tt + 2 * HALO, BRANCH_W), F32), pltpu.VMEM((tt + 2 * HALO, BRANCH_W), F32),
                        pltpu.VMEM((7, tt + 2 * HALO - 8, BRANCH_W), F32)],
        input_output_aliases={18: 0},
        compiler_params=_params(("arbitrary",)),
        name="conv_branches",
    )(z, z, z, z, z, z, z, z, z, z, z, z, z, short_w, dw_w, dw_b.reshape(1, BRANCH_W),
      ln_g.reshape(1, BRANCH_W), ln_b.reshape(1, BRANCH_W), br)


Z_TN = 2 * AKV
assert OFF_K % Z_TN == 0 and OFF_HQ - OFF_K == Z_TN and Z_COLS % Z_TN == 0


def _z_col_block(j):
    kv = OFF_K // Z_TN
    return jnp.where(j < kv, j, jnp.where(j == kv, Z_COLS // Z_TN - 1, j - 1))


def kernel(x, c, ctx, c_ctx, w_mod, b_mod, norm_g, ffn_w1, ffn_w2, w_in, hgrn_lb_logits, hgrn_norm_g, qk_norm_g,
           short_conv_w, conf_dw_w, conf_dw_b, conf_ln_g, conf_ln_b, w_branch, w_out):
    d = Dims(nb=x.shape[0], t_lat=x.shape[1], t_ctx=ctx.shape[1])
    assert d.nb + 1 <= MOD_ROWS and d.t_lat % d.t_ctx == 0 and d.t_ctx % HGRN_CHUNK == 0 and d.t_ctx % HALO == 0
    assert d.t_lat % GRID_W == 0 and d.t_ctx >= HALO

    cc = jnp.concatenate([c, c_ctx[None], jnp.zeros((MOD_ROWS - d.nb - 1, D_MODEL), F32)], axis=0)
    modv_all = mod_vectors(cc, w_mod, b_mod).reshape(DEPTH, MOD_ROWS, 1, N_MOD * D_MODEL)

    s = jnp.concatenate([ctx, x], axis=1).reshape(d.m_all, D_MODEL)
    w1_bf = ffn_w1.astype(BF16)
    w2_bf = ffn_w2.astype(BF16)
    w_in_bf = w_in.astype(BF16)
    w_branch_bf = w_branch.astype(BF16)
    w_out_bf = w_out.astype(BF16)
    gla_consts = _gla_constants()
    tables = _rope_tables(d)

    g6_all = norm_g.reshape(DEPTH, 6, 1, D_MODEL)
    h = normmod(d, s, g6_all[0], modv_all[0], 0, 0)
    for l in range(DEPTH):
        last = l == DEPTH - 1
        modv = modv_all[l]
        g6 = g6_all[l]

        y = matmul(swiglu_up(h, w1_bf, (l, 0)), w2_bf, (l, 0), D_MODEL, 1024, BF16)
        s, h = resid_normmod(d, s, y, g6, modv, 1, 2, FFN_RESIDUAL, 2, 3)

        z = matmul(h, w_in_bf, (l,), Z_COLS, Z_TN, F32, out_col_block=_z_col_block)
        br = hgrn_branch(d, z, hgrn_lb_logits, hgrn_norm_g[l], l, gla_consts)
        br = attention_branch(d, z, br, qk_norm_g[l], tables)
        br = conv_branches(d, z, br, short_conv_w[l], conf_dw_w[l], conf_dw_b[l], conf_ln_g[l], conf_ln_b[l])
        y = matmul(gated_merge(h, br, w_in_bf, w_branch_bf, l), w_out_bf, (l,), D_MODEL, 1024, BF16)

        if last:
            s, h = resid_normmod_latent(d, s, y, g6, modv, 3, 5, 1.0, 4, 6)
        else:
            s, h = resid_normmod(d, s, y, g6, modv, 3, 5, 1.0, 4, 6)
        y = matmul(swiglu_up(h, w1_bf, (l, 1)), w2_bf, (l, 1), D_MODEL, 1024, BF16)
        if last:
            s = resid_last_latent(d, s, y, g6, modv, 5, 8, FFN_RESIDUAL)
        else:
            s, h = resid_normmod(d, s, y, g6, modv, 5, 8, FFN_RESIDUAL, 0, 0, g6_all[l + 1], modv_all[l + 1])
    return s.reshape(d.nb, d.t_lat, D_MODEL)
```

```python
import functools
from typing import NamedTuple

import numpy as np
import jax
import jax.numpy as jnp
from jax import lax
from jax.experimental import pallas as pl
from jax.experimental.pallas import tpu as pltpu

D_MODEL = 4096
DEPTH = 2
GRID_W = 64
N_BRANCH = 4
BRANCH_W = D_MODEL // 4
HGRN_DK = 128
HGRN_DV = 128
HGRN_HEADS = BRANCH_W // HGRN_DV
HGRN_CHUNK = 64
HEAD_DIM = 128
ATTN_Q_HEADS = BRANCH_W // HEAD_DIM
ATTN_KV_HEADS = ATTN_Q_HEADS // 4
ATTN_GROUP = ATTN_Q_HEADS // ATTN_KV_HEADS
ROPE_AXIS_DIM = HEAD_DIM // 2
ROPE_THETA = 10000.0
SHORT_CONV_W = 3
CONF_CONV_W = 31
FFN_DIM = D_MODEL
FFN_RESIDUAL = 0.5
N_MOD = 9
EPS = 1e-6
LB_FLOOR = 1e-30
LOG2_E = 1.4426950408889634

HK = HGRN_HEADS * HGRN_DK
HV = HGRN_HEADS * HGRN_DV
AQ = ATTN_Q_HEADS * HEAD_DIM
AKV = ATTN_KV_HEADS * HEAD_DIM
OFF_K = 2 * HK + HV
OFF_HQ = OFF_K + 2 * AKV
OFF_GATE = OFF_HQ + HK + HV + AQ + 3 * BRANCH_W + 2 * BRANCH_W
N_IN_COLS = OFF_GATE + N_BRANCH * D_MODEL

ZF, ZB, ZI, ZHQ, ZHG, ZAQ, ZSB, ZSC, ZSU, ZGA, ZGG = (k * BRANCH_W for k in range(11))
ZK = 11 * BRANCH_W
ZV = ZK + AKV
Z_COLS = ZV + AKV
assert Z_COLS == OFF_GATE

MOD_ROWS = 16
HALO = 16
assert CONF_CONV_W // 2 < HALO and HALO % 8 == 0
LANES = 128
MXU_WIDTH = 256

V7X_VMEM_BYTES = 64 * 1024 * 1024
VMEM_LIMIT = V7X_VMEM_BYTES - 8 * 1024 * 1024

BF16 = jnp.bfloat16
F32 = jnp.float32


class Dims(NamedTuple):
    nb: int
    t_lat: int
    t_ctx: int

    @property
    def pb(self):
        return self.t_lat + self.t_ctx

    @property
    def m_all(self):
        return self.nb * self.pb

    @property
    def tiles_per_batch(self):
        return self.pb // self.t_ctx


def _params(sem):
    return pltpu.CompilerParams(dimension_semantics=sem, vmem_limit_bytes=VMEM_LIMIT)


def _mod_kernel(cc_ref, w_ref, b_ref, o_ref):
    a = cc_ref[...]
    a = (a * jax.nn.sigmoid(a)).astype(BF16)
    o_ref[...] = jnp.dot(a, w_ref[...].astype(BF16), preferred_element_type=F32) + b_ref[...]


def mod_vectors(cc, w_mod, b_mod):
    tn = 512
    n = N_MOD * D_MODEL
    return pl.pallas_call(
        _mod_kernel,
        grid=(DEPTH, n // tn),
        in_specs=[pl.BlockSpec((MOD_ROWS, D_MODEL), lambda l, j: (0, 0)),
                  pl.BlockSpec((None, D_MODEL, tn), lambda l, j: (l, 0, j)),
                  pl.BlockSpec((None, 1, tn), lambda l, j: (l, 0, j))],
        out_specs=pl.BlockSpec((None, MOD_ROWS, tn), lambda l, j: (l, 0, j)),
        out_shape=jax.ShapeDtypeStruct((DEPTH, MOD_ROWS, n), F32),
        compiler_params=_params(("arbitrary", "arbitrary")),
        name="mod_vectors",
    )(cc, w_mod, b_mod.reshape(DEPTH, 1, n))


def _mod_spec(d, chunk):
    tpb = d.tiles_per_batch
    return pl.BlockSpec((None, 1, D_MODEL), lambda i: (jnp.where(i % tpb == 0, d.nb, i // tpb), 0, chunk))


def _g_spec(k):
    return pl.BlockSpec((None, 1, D_MODEL), lambda i: (k, 0, 0))


def _row_spec(d):
    return pl.BlockSpec((d.t_ctx, D_MODEL), lambda i: (i, 0))


def _lat_in_spec(d, chunk=None):
    tpb = d.tiles_per_batch
    if chunk is None:
        return pl.BlockSpec((d.t_ctx, D_MODEL), lambda b, j: (b * tpb + 1 + j, 0))
    return pl.BlockSpec((None, 1, D_MODEL), lambda b, j: (b, 0, chunk))


def _rms(x):
    return x * lax.rsqrt(jnp.mean(x * x, axis=-1, keepdims=True) + EPS)


def _normmod_kernel(s_ref, g_ref, sh_ref, sc_ref, o_ref):
    y = _rms(s_ref[...]) * g_ref[...]
    o_ref[...] = (y * (1 + sc_ref[...]) + sh_ref[...]).astype(o_ref.dtype)


def normmod(d, s, g6, modv, g_idx, shift_idx):
    return pl.pallas_call(
        _normmod_kernel,
        grid=(d.m_all // d.t_ctx,),
        in_specs=[_row_spec(d), _g_spec(g_idx), _mod_spec(d, shift_idx), _mod_spec(d, shift_idx + 1)],
        out_specs=_row_spec(d),
        out_shape=jax.ShapeDtypeStruct((d.m_all, D_MODEL), BF16),
        compiler_params=_params(("arbitrary",)),
        name="normmod",
    )(s, g6, modv, modv)


def _resid_kernel(s_ref, y_ref, gpost_ref, gate_ref, gpre_ref, sh_ref, sc_ref, so_ref, ho_ref, *, coef):
    yn = _rms(y_ref[...].astype(F32)) * gpost_ref[...]
    s = s_ref[...] + (coef * gate_ref[...]) * yn
    so_ref[...] = s
    h = _rms(s) * gpre_ref[...]
    ho_ref[...] = (h * (1 + sc_ref[...]) + sh_ref[...]).astype(ho_ref.dtype)


def _resid_last_kernel(s_ref, y_ref, gpost_ref, gate_ref, so_ref, *, coef):
    yn = _rms(y_ref[...].astype(F32)) * gpost_ref[...]
    so_ref[...] = s_ref[...] + (coef * gate_ref[...]) * yn


def resid_normmod(d, s, y, g6, modv, gpost_idx, gate_idx, coef, gpre_idx, shift_idx, g6_pre=None, modv_pre=None):
    g6_pre = g6 if g6_pre is None else g6_pre
    modv_pre = modv if modv_pre is None else modv_pre
    return pl.pallas_call(
        functools.partial(_resid_kernel, coef=coef),
        grid=(d.m_all // d.t_ctx,),
        in_specs=[_row_spec(d), _row_spec(d), _g_spec(gpost_idx), _mod_spec(d, gate_idx),
                  _g_spec(gpre_idx), _mod_spec(d, shift_idx), _mod_spec(d, shift_idx + 1)],
        out_specs=[_row_spec(d), _row_spec(d)],
        out_shape=[jax.ShapeDtypeStruct((d.m_all, D_MODEL), F32), jax.ShapeDtypeStruct((d.m_all, D_MODEL), BF16)],
        compiler_params=_params(("arbitrary",)),
        name="resid_normmod",
    )(s, y, g6, modv, g6_pre, modv_pre, modv_pre)


def resid_normmod_latent(d, s, y, g6, modv, gpost_idx, gate_idx, coef, gpre_idx, shift_idx):
    nl = d.t_lat // d.t_ctx
    g_spec = lambda k: pl.BlockSpec((None, 1, D_MODEL), lambda b, j: (k, 0, 0))
    dense = pl.BlockSpec((d.t_ctx, D_MODEL), lambda b, j: (b * nl + j, 0))
    return pl.pallas_call(
        functools.partial(_resid_kernel, coef=coef),
        grid=(d.nb, nl),
        in_specs=[_lat_in_spec(d), _lat_in_spec(d), g_spec(gpost_idx), _lat_in_spec(d, gate_idx),
                  g_spec(gpre_idx), _lat_in_spec(d, shift_idx), _lat_in_spec(d, shift_idx + 1)],
        out_specs=[dense, dense],
        out_shape=[jax.ShapeDtypeStruct((d.nb * d.t_lat, D_MODEL), F32),
                   jax.ShapeDtypeStruct((d.nb * d.t_lat, D_MODEL), BF16)],
        compiler_params=_params(("arbitrary", "arbitrary")),
        name="resid_normmod_latent",
    )(s, y, g6, modv, g6, modv, modv)


def resid_last_latent(d, s, y, g6, modv, gpost_idx, gate_idx, coef):
    nl = d.t_lat // d.t_ctx
    dense = pl.BlockSpec((d.t_ctx, D_MODEL), lambda b, j: (b * nl + j, 0))
    return pl.pallas_call(
        functools.partial(_resid_last_kernel, coef=coef),
        grid=(d.nb, nl),
        in_specs=[dense, dense, pl.BlockSpec((None, 1, D_MODEL), lambda b, j: (gpost_idx, 0, 0)),
                  pl.BlockSpec((None, 1, D_MODEL), lambda b, j: (b, 0, gate_idx))],
        out_specs=dense,
        out_shape=jax.ShapeDtypeStruct((d.nb * d.t_lat, D_MODEL), F32),
        compiler_params=_params(("arbitrary", "arbitrary")),
        name="resid_last_latent",
    )(s, y, g6, modv)


def _row_tile(m):
    return next(t for t in (1024, 512, 256, 128) if m % t == 0)


def _mm_kernel(x_ref, w_ref, o_ref):
    o_ref[...] = jnp.dot(x_ref[...], w_ref[...], preferred_element_type=F32).astype(o_ref.dtype)


def _w_spec(lead, k, tn, col):
    return pl.BlockSpec((None,) * len(lead) + (k, tn), lambda *g: tuple(lead) + (0, col(*g)))


def matmul(x, w, lead, ncols, tn, out_dtype, out_col_block=lambda j: j):
    m, k = x.shape
    tm = _row_tile(m)
    return pl.pallas_call(
        _mm_kernel,
        grid=(m // tm, ncols // tn),
        in_specs=[pl.BlockSpec((tm, k), lambda i, j: (i, 0)),
                  _w_spec(lead, k, tn, lambda i, j: j)],
        out_specs=pl.BlockSpec((tm, tn), lambda i, j: (i, out_col_block(j))),
        out_shape=jax.ShapeDtypeStruct((m, ncols), out_dtype),
        compiler_params=_params(("arbitrary", "arbitrary")),
        name="matmul",
    )(x, w)


def _col_halves(n):
    half = n // 2
    return (slice(0, half), slice(half, n)) if half % MXU_WIDTH == 0 else (slice(0, n),)


def _swiglu_kernel(x_ref, wa_ref, wb_ref, o_ref):
    for cs in _col_halves(o_ref.shape[1]):
        a = jnp.dot(x_ref[...], wa_ref[:, cs], preferred_element_type=F32)
        b = jnp.dot(x_ref[...], wb_ref[:, cs], preferred_element_type=F32)
        o_ref[:, cs] = (a * jax.nn.sigmoid(a) * b).astype(o_ref.dtype)


def swiglu_up(h, w1, lead):
    m = h.shape[0]
    tm = _row_tile(m)
    tn = 512
    nb = FFN_DIM // tn
    return pl.pallas_call(
        _swiglu_kernel,
        grid=(m // tm, nb),
        in_specs=[pl.BlockSpec((tm, D_MODEL), lambda i, j: (i, 0)),
                  _w_spec(lead, D_MODEL, tn, lambda i, j: j),
                  _w_spec(lead, D_MODEL, tn, lambda i, j: j + nb)],
        out_specs=pl.BlockSpec((tm, tn), lambda i, j: (i, j)),
        out_shape=jax.ShapeDtypeStruct((m, FFN_DIM), BF16),
        compiler_params=_params(("arbitrary", "arbitrary")),
        name="swiglu_up",
    )(h, w1, w1)


def _merge_kernel(h_ref, br_ref, wg_ref, wb_ref, o_ref, acc_ref):
    b = pl.program_id(2)

    @pl.when(b == 0)
    def _():
        acc_ref[...] = jnp.zeros_like(acc_ref)

    for cs in _col_halves(o_ref.shape[1]):
        g = jnp.dot(h_ref[...], wg_ref[:, cs], preferred_element_type=F32)
        p = jnp.dot(br_ref[...], wb_ref[:, cs], preferred_element_type=F32)
        acc_ref[:, cs] += jax.nn.sigmoid(g) * p

    @pl.when(b == N_BRANCH - 1)
    def _():
        o_ref[...] = acc_ref[...].astype(o_ref.dtype)


def gated_merge(h, br, w_in, w_branch, layer):
    m = h.shape[0]
    tm = _row_tile(m)
    tn = 512
    nb = D_MODEL // tn
    gate0 = OFF_GATE // tn
    return pl.pallas_call(
        _merge_kernel,
        grid=(m // tm, nb, N_BRANCH),
        in_specs=[pl.BlockSpec((tm, D_MODEL), lambda i, j, b: (i, 0)),
                  pl.BlockSpec((tm, BRANCH_W), lambda i, j, b: (i, b)),
                  _w_spec((layer,), D_MODEL, tn, lambda i, j, b: gate0 + b * nb + j),
                  pl.BlockSpec((None, None, BRANCH_W, tn), lambda i, j, b: (layer, b, 0, j))],
        out_specs=pl.BlockSpec((tm, tn), lambda i, j, b: (i, j)),
        out_shape=jax.ShapeDtypeStruct((m, D_MODEL), BF16),
        scratch_shapes=[pltpu.VMEM((tm, tn), F32)],
        compiler_params=_params(("arbitrary", "arbitrary", "arbitrary")),
        name="gated_merge",
    )(h, br, w_in, w_branch)


_GLA_LEVELS = (8, 16, 32)
GLA_HEADS_PER_STEP = 2


def _gla_constants():
    c = HGRN_CHUNK
    i = np.arange(c)[:, None]
    t = np.arange(c)[None, :]
    flip = lambda m: m[::-1, ::-1]
    mats = [t <= i]
    masks = []
    for s in _GLA_LEVELS:
        blk = i // s
        right = blk % 2 == 1
        mats.append(right & (t > blk * s) & (t <= i))
        mats.append(~right & (t > i) & (t <= (blk + 1) * s))
        masks.append(right & (t // s == blk - 1))
    mats.append(t > i)
    g_f = np.concatenate(mats, 0).astype(np.float32)
    g_b = np.concatenate([flip(m) for m in mats], 0).astype(np.float32)
    m_f = np.stack(masks).astype(np.float32)
    m_b = np.stack([flip(m) for m in masks]).astype(np.float32)
    tile3 = lambda g: jnp.asarray(np.tile(g, (1, 3)), BF16)
    return tile3(g_f), tile3(g_b), jnp.asarray(m_f), jnp.asarray(m_b)


def _nt_dot(a, b):
    return lax.dot_general(a, b, (((1,), (1,)), ((), ())), preferred_element_type=F32)


def _gla_gates(a, lb_floor, one_minus_lb, gmat):
    f = lb_floor + one_minus_lb * jax.nn.sigmoid(a)
    lf = jnp.log(f)
    hi = lf.astype(BF16)
    r1 = lf - hi.astype(F32)
    mid = r1.astype(BF16)
    lo = (r1 - mid.astype(F32)).astype(BF16)
    ex = jnp.dot(gmat, jnp.concatenate([hi, mid, lo], axis=0), preferred_element_type=F32)
    return 1.0 - f, ex


def _gla_direct_terms(q, k, cum, reverse):
    c = HGRN_CHUNK
    nsub = c // 8
    q3 = q.reshape(nsub, 8, LANES)
    k3 = k.reshape(nsub, 8, LANES)
    c3 = cum.reshape(nsub, 8, LANES)
    ii = lax.broadcasted_iota(jnp.int32, (nsub, 8, LANES), 1)
    ws = []
    for jj in range(8):
        kj = jnp.broadcast_to(k3[:, jj:jj + 1, :], (nsub, 8, LANES))
        cj = jnp.broadcast_to(c3[:, jj:jj + 1, :], (nsub, 8, LANES))
        w = q3 * kj * jnp.exp(jnp.minimum(c3 - cj, 0.0))
        keep = (ii <= jj) if reverse else (ii >= jj)
        ws.append(jnp.where(keep, w, 0.0).reshape(c, LANES).astype(BF16))
    return jnp.concatenate(ws, axis=0)


def _gla_chunk(q, k, v, ex, masks, st, reverse):
    c = HGRN_CHUNK
    nsub = c // 8
    g_rows = (2 + 2 * len(_GLA_LEVELS)) * c
    cum = ex[0:c]
    vb16 = v.astype(BF16)

    p = jnp.zeros((c, c), F32)
    for li in range(len(_GLA_LEVELS)):
        ea = ex[c * (1 + 2 * li):c * (2 + 2 * li)]
        eb = ex[c * (2 + 2 * li):c * (3 + 2 * li)]
        qa = (q * jnp.exp(jnp.minimum(ea, 0.0))).astype(BF16)
        kb = (k * jnp.exp(jnp.minimum(eb, 0.0))).astype(BF16)
        p = p + masks[li] * _nt_dot(qa, kb)
    o = jnp.dot(p.astype(BF16), vb16, preferred_element_type=F32)

    rs = jnp.dot(_gla_direct_terms(q, k, cum, reverse), jnp.ones((LANES, LANES), BF16), preferred_element_type=F32)
    v3 = v.reshape(nsub, 8, LANES)
    o3 = o.reshape(nsub, 8, LANES)
    for jj in range(8):
        vj = jnp.broadcast_to(v3[:, jj:jj + 1, :], (nsub, 8, LANES))
        o3 = o3 + rs[jj * c:(jj + 1) * c].reshape(nsub, 8, LANES) * vj
    o = o3.reshape(c, LANES)

    o = o + _nt_dot((q * jnp.exp(cum)).astype(BF16), st.astype(BF16))
    ke = (k * jnp.exp(ex[g_rows - c:g_rows])).astype(BF16)
    total = cum[0:1] if reverse else cum[c - 1:c]
    st_new = jnp.exp(total) * st + jnp.dot(v.T.astype(BF16), ke, preferred_element_type=F32)
    return o, st_new


def _gla_kernel(af_ref, ab_ref, v_ref, q_ref, g_ref, lbl_ref, ng_ref, gf_ref, gb_ref, mf_ref, mb_ref,
                o_ref, of_ref, ob_ref, sf_ref, sb_ref, *, layer, t_ctx, t_lat):
    c = HGRN_CHUNK
    logits = lbl_ref[...]
    e = jnp.exp(logits - jnp.max(logits, axis=0, keepdims=True))
    pr = e / jnp.sum(e, axis=0, keepdims=True)
    csum = pr[0]
    for dd in range(1, layer + 1):
        csum = csum + pr[dd]
    lb = csum - pr[0]
    lb_floor = jnp.maximum(lb, LB_FLOOR)
    one_minus_lb = 1.0 - lb
    masks_f = [mf_ref[li] for li in range(len(_GLA_LEVELS))]
    masks_b = [mb_ref[li] for li in range(len(_GLA_LEVELS))]
    gmat_f = gf_ref[...]
    gmat_b = gb_ref[...]

    sf_ref[...] = jnp.zeros_like(sf_ref)
    sb_ref[...] = jnp.zeros_like(sb_ref)

    def scan(row0, nchunks):
        def body(ci, carry):
            rf = pl.multiple_of(row0 + ci * c, c)
            rb = pl.multiple_of(row0 + (nchunks - 1 - ci) * c, c)
            lanes = [slice(hh * LANES, (hh + 1) * LANES) for hh in range(GLA_HEADS_PER_STEP)]
            q_f, v_f = q_ref[pl.ds(rf, c), :], v_ref[pl.ds(rf, c), :]
            q_b, v_b = q_ref[pl.ds(rb, c), :], v_ref[pl.ds(rb, c), :]
            st_f = [sf_ref[hh] for hh in range(GLA_HEADS_PER_STEP)]
            st_b = [sb_ref[hh] for hh in range(GLA_HEADS_PER_STEP)]
            k_f, ex_f = _gla_gates(af_ref[pl.ds(rf, c), :], lb_floor[0:1], one_minus_lb[0:1], gmat_f)
            k_b, ex_b = _gla_gates(ab_ref[pl.ds(rb, c), :], lb_floor[1:2], one_minus_lb[1:2], gmat_b)
            outs = []
            for hh, ln in enumerate(lanes):
                outs.append(_gla_chunk(q_f[:, ln], k_f[:, ln], v_f[:, ln], ex_f[:, ln], masks_f, st_f[hh], False))
                outs.append(_gla_chunk(q_b[:, ln], k_b[:, ln], v_b[:, ln], ex_b[:, ln], masks_b, st_b[hh], True))
            for hh, ln in enumerate(lanes):
                (o_f, s_f), (o_b, s_b) = outs[2 * hh], outs[2 * hh + 1]
                of_ref[pl.ds(rf, c), ln] = o_f
                ob_ref[pl.ds(rb, c), ln] = o_b
                sf_ref[hh] = s_f
                sb_ref[hh] = s_b
            return carry
        lax.fori_loop(0, nchunks, body, 0, unroll=4)

    scan(0, t_ctx // c)
    scan(t_ctx, t_lat // c)

    for hh in range(GLA_HEADS_PER_STEP):
        ln = slice(hh * LANES, (hh + 1) * LANES)
        o = of_ref[:, ln] + ob_ref[:, ln]
        y = _rms(o) * ng_ref[...] * jax.nn.sigmoid(g_ref[:, ln])
        o_ref[:, ln] = y.astype(o_ref.dtype)


def hgrn_branch(d, z, lb_logits, norm_g, layer, consts):
    g_f, g_b, m_f, m_b = consts
    w = GLA_HEADS_PER_STEP * LANES
    col = lambda base: (lambda b, h: (b, base // w + h))
    zspec = lambda base: pl.BlockSpec((d.pb, w), col(base))
    const2 = lambda shape: pl.BlockSpec(shape, lambda b, h: (0,) * len(shape))
    state = pltpu.VMEM((GLA_HEADS_PER_STEP, HGRN_DV, HGRN_DK), F32)
    return pl.pallas_call(
        functools.partial(_gla_kernel, layer=layer, t_ctx=d.t_ctx, t_lat=d.t_lat),
        grid=(d.nb, HGRN_HEADS // GLA_HEADS_PER_STEP),
        in_specs=[zspec(ZF), zspec(ZB), zspec(ZI), zspec(ZHQ), zspec(ZHG),
                  pl.BlockSpec((DEPTH, 2, w), lambda b, h: (0, 0, h)),
                  const2((1, HGRN_DV)), const2(g_f.shape), const2(g_b.shape), const2(m_f.shape), const2(m_b.shape)],
        out_specs=pl.BlockSpec((d.pb, w), lambda b, h: (b, h)),
        out_shape=jax.ShapeDtypeStruct((d.m_all, D_MODEL), BF16),
        scratch_shapes=[pltpu.VMEM((d.pb, w), F32), pltpu.VMEM((d.pb, w), F32), state, state],
        compiler_params=_params(("arbitrary", "arbitrary")),
        name="hgrn_branch",
    )(z, z, z, z, z, lb_logits, norm_g.reshape(1, HGRN_DV), g_f, g_b, m_f, m_b)


def _rope_tables(d):
    half = ROPE_AXIS_DIM // 2
    tok = np.arange(d.t_lat)
    inv = ROPE_THETA ** (-np.arange(0, ROPE_AXIS_DIM, 2, dtype=np.float32) / ROPE_AXIS_DIM)
    ang_r = (tok // GRID_W).astype(np.float32)[:, None] * inv.astype(np.float32)
    ang_c = (tok % GRID_W).astype(np.float32)[:, None] * inv.astype(np.float32)
    ang = jnp.asarray(np.concatenate([ang_r, ang_r, ang_c, ang_c], axis=1), F32)
    first = jnp.asarray((np.arange(HEAD_DIM) % ROPE_AXIS_DIM) < half)
    cos = jnp.cos(ang)
    sin = jnp.sin(ang)
    sin_a = jnp.where(first, -sin, 0.0)
    sin_b = jnp.where(first, 0.0, sin)
    pad = lambda t, fill: jnp.concatenate([jnp.full((d.t_ctx, HEAD_DIM), fill, F32), t], axis=0)
    return pad(cos, 1.0), pad(sin_a, 0.0), pad(sin_b, 0.0)


def _rope(u, cos, sin_a, sin_b):
    half = ROPE_AXIS_DIM // 2
    return u * cos + pltpu.roll(u, HEAD_DIM - half, 1) * sin_a + pltpu.roll(u, half, 1) * sin_b


def _attn_kernel(q_ref, k_ref, v_ref, g_ref, ck_ref, sak_ref, sbk_ref, cq_ref, saq_ref, sbq_ref, br_ref,
                 o_ref, kbuf, vbuf, *, t_ctx):
    del br_ref
    qi = pl.program_id(2)
    tq = q_ref.shape[0]
    scale = HEAD_DIM ** -0.5

    @pl.when(qi == 0)
    def _():
        kn = _rms(k_ref[...]) * g_ref[1:2, :]
        kbuf[...] = _rope(kn, ck_ref[...], sak_ref[...], sbk_ref[...]).astype(BF16)
        vbuf[...] = v_ref[...].astype(BF16)

    q = q_ref[...]
    cq, saq, sbq = cq_ref[...], saq_ref[...], sbq_ref[...]
    qs = []
    for gi in range(ATTN_GROUP):
        qn = _rms(q[:, gi * HEAD_DIM:(gi + 1) * HEAD_DIM]) * g_ref[0:1, :]
        qs.append(_rope(qn, cq, saq, sbq).astype(BF16))
    q4 = jnp.concatenate(qs, axis=0)

    def attend(keys, vals):
        s = _nt_dot(q4, keys)
        m = jnp.max(s, axis=-1, keepdims=True)
        p = jnp.exp2((s - m) * (scale * LOG2_E))
        l = jnp.sum(p, axis=-1, keepdims=True)
        o = jnp.dot(p.astype(BF16), vals, preferred_element_type=F32) / l
        for gi in range(ATTN_GROUP):
            o_ref[:, gi * HEAD_DIM:(gi + 1) * HEAD_DIM] = o[gi * tq:(gi + 1) * tq].astype(o_ref.dtype)

    @pl.when(qi == 0)
    def _():
        attend(kbuf[0:t_ctx, :], vbuf[0:t_ctx, :])

    @pl.when(qi > 0)
    def _():
        attend(kbuf[...], vbuf[...])


def attention_branch(d, z, br, qk_norm_g, tables):
    cos, sin_a, sin_b = tables
    tq = d.t_ctx
    nq = d.tiles_per_batch
    gw = ATTN_GROUP * HEAD_DIM
    kspec = lambda base: pl.BlockSpec((d.pb, HEAD_DIM), lambda b, h, qi: (b, base // HEAD_DIM + h))
    tab_k = pl.BlockSpec((d.pb, HEAD_DIM), lambda b, h, qi: (0, 0))
    tab_q = pl.BlockSpec((tq, HEAD_DIM), lambda b, h, qi: (qi, 0))
    return pl.pallas_call(
        functools.partial(_attn_kernel, t_ctx=d.t_ctx),
        grid=(d.nb, ATTN_KV_HEADS, nq),
        in_specs=[pl.BlockSpec((tq, gw), lambda b, h, qi: (b * nq + qi, ZAQ // gw + h)),
                  kspec(ZK), kspec(ZV),
                  pl.BlockSpec((2, HEAD_DIM), lambda b, h, qi: (0, 0)),
                  tab_k, tab_k, tab_k, tab_q, tab_q, tab_q,
                  pl.BlockSpec(memory_space=pl.ANY)],
        out_specs=pl.BlockSpec((tq, gw), lambda b, h, qi: (b * nq + qi, BRANCH_W // gw + h)),
        out_shape=jax.ShapeDtypeStruct((d.m_all, D_MODEL), BF16),
        scratch_shapes=[pltpu.VMEM((d.pb, HEAD_DIM), BF16), pltpu.VMEM((d.pb, HEAD_DIM), BF16)],
        input_output_aliases={10: 0},
        compiler_params=_params(("arbitrary", "arbitrary", "arbitrary")),
        name="attention_branch",
    )(z, z, z, qk_norm_g, cos, sin_a, sin_b, cos, sin_a, sin_b, br)


def _conv_kernel(sb_ref, sc_ref, su_ref, ga_ref, gg_ref,
                 scp_ref, sup_ref, gap_ref, ggp_ref, scn_ref, sun_ref, gan_ref, ggn_ref,
                 ws_ref, wc_ref, bc_ref, lg_ref, lb_ref, br_ref, o_ref, ps_ref, pc_ref, pcs_ref, *, tiles_per_batch):
    del br_ref
    tt = sb_ref.shape[0]
    within = pl.program_id(0) % tiles_per_batch
    prev_ok = (within >= 2).astype(F32)
    next_ok = jnp.logical_and(within >= 1, within <= tiles_per_batch - 2).astype(F32)

    glu = lambda a, g: a * jax.nn.sigmoid(g)
    ps_ref[0:HALO, :] = scp_ref[...] * sup_ref[...] * prev_ok
    ps_ref[HALO:HALO + tt, :] = sc_ref[...] * su_ref[...]
    ps_ref[HALO + tt:2 * HALO + tt, :] = scn_ref[...] * sun_ref[...] * next_ok
    pc_ref[0:HALO, :] = glu(gap_ref[...], ggp_ref[...]) * prev_ok
    pc_ref[HALO:HALO + tt, :] = glu(ga_ref[...], gg_ref[...])
    pc_ref[HALO + tt:2 * HALO + tt, :] = glu(gan_ref[...], ggn_ref[...]) * next_ok

    acc = jnp.zeros((tt, BRANCH_W), F32)
    for tau in range(SHORT_CONV_W):
        acc = acc + ps_ref[pl.ds(HALO - SHORT_CONV_W // 2 + tau, tt), :] * ws_ref[tau:tau + 1, :]
    o_ref[:, 0:BRANCH_W] = (sb_ref[...] * acc).astype(o_ref.dtype)

    n = tt + 2 * HALO - 8
    for r in range(1, 8):
        pcs_ref[r - 1] = pc_ref[pl.ds(r, n), :]
    acc = jnp.zeros((tt, BRANCH_W), F32)
    for tau in range(CONF_CONV_W):
        off = HALO - CONF_CONV_W // 2 + tau
        base = (off // 8) * 8
        tap = pc_ref[pl.ds(base, tt), :] if off % 8 == 0 else pcs_ref[off % 8 - 1, pl.ds(base, tt), :]
        acc = acc + tap * wc_ref[tau:tau + 1, :]
    u = acc + bc_ref[...]
    uc = u - jnp.mean(u, axis=-1, keepdims=True)
    y = uc * lax.rsqrt(jnp.mean(uc * uc, axis=-1, keepdims=True) + EPS) * lg_ref[...] + lb_ref[...]
    o_ref[:, BRANCH_W:2 * BRANCH_W] = (y * jax.nn.sigmoid(y)).astype(o_ref.dtype)


def conv_branches(d, z, br, short_w, dw_w, dw_b, ln_g, ln_b):
    tt = d.t_ctx
    per = tt // HALO
    last = d.m_all // HALO - 1
    cur = lambda base: pl.BlockSpec((tt, BRANCH_W), lambda i: (i, base // BRANCH_W))
    prv = lambda base: pl.BlockSpec((HALO, BRANCH_W), lambda i: (jnp.maximum(i * per - 1, 0), base // BRANCH_W))
    nxt = lambda base: pl.BlockSpec((HALO, BRANCH_W), lambda i: (jnp.minimum((i + 1) * per, last), base // BRANCH_W))
    full = lambda shape: pl.BlockSpec(shape, lambda i: (0,) * len(shape))
    row = full((1, BRANCH_W))
    return pl.pallas_call(
        functools.partial(_conv_kernel, tiles_per_batch=d.tiles_per_batch),
        grid=(d.m_all // tt,),
        in_specs=[cur(ZSB), cur(ZSC), cur(ZSU), cur(ZGA), cur(ZGG),
                  prv(ZSC), prv(ZSU), prv(ZGA), prv(ZGG), nxt(ZSC), nxt(ZSU), nxt(ZGA), nxt(ZGG),
                  full((SHORT_CONV_W, BRANCH_W)), full((CONF_CONV_W, BRANCH_W)), row, row, row,
                  pl.BlockSpec(memory_space=pl.ANY)],
        out_specs=pl.BlockSpec((tt, 2 * BRANCH_W), lambda i: (i, 1)),
        out_shape=jax.ShapeDtypeStruct((d.m_all, D_MODEL), BF16),
        scratch_shapes=[pltpu.VMEM((tt + 2 * HALO, BRANCH_W), F32), pltpu.VMEM((tt + 2 * HALO, BRANCH_W), F32),
                        pltpu.VMEM((7, tt + 2 * HALO - 8, BRANCH_W), F32)],
        input_output_aliases={18: 0},
        compiler_params=_params(("arbitrary",)),
        name="conv_branches",
    )(z, z, z, z, z, z, z, z, z, z, z, z, z, short_w, dw_w, dw_b.reshape(1, BRANCH_W),
      ln_g.reshape(1, BRANCH_W), ln_b.reshape(1, BRANCH_W), br)


Z_TN = 2 * AKV
assert OFF_K % Z_TN == 0 and OFF_HQ - OFF_K == Z_TN and Z_COLS % Z_TN == 0


def _z_col_block(j):
    kv = OFF_K // Z_TN
    return jnp.where(j < kv, j, jnp.where(j == kv, Z_COLS // Z_TN - 1, j - 1))


def kernel(x, c, ctx, c_ctx, w_mod, b_mod, norm_g, ffn_w1, ffn_w2, w_in, hgrn_lb_logits, hgrn_norm_g, qk_norm_g,
           short_conv_w, conf_dw_w, conf_dw_b, conf_ln_g, conf_ln_b, w_branch, w_out):
    d = Dims(nb=x.shape[0], t_lat=x.shape[1], t_ctx=ctx.shape[1])
    assert d.nb + 1 <= MOD_ROWS and d.t_lat % d.t_ctx == 0 and d.t_ctx % HGRN_CHUNK == 0 and d.t_ctx % HALO == 0
    assert d.t_lat % GRID_W == 0 and d.t_ctx >= HALO

    cc = jnp.concatenate([c, c_ctx[None], jnp.zeros((MOD_ROWS - d.nb - 1, D_MODEL), F32)], axis=0)
    modv_all = mod_vectors(cc, w_mod, b_mod).reshape(DEPTH, MOD_ROWS, 1, N_MOD * D_MODEL)

    s = jnp.concatenate([ctx, x], axis=1).reshape(d.m_all, D_MODEL)
    w1_bf = ffn_w1.astype(BF16)
    w2_bf = ffn_w2.astype(BF16)
    w_in_bf = w_in.astype(BF16)
    w_branch_bf = w_branch.astype(BF16)
    w_out_bf = w_out.astype(BF16)
    gla_consts = _gla_constants()
    tables = _rope_tables(d)

    g6_all = norm_g.reshape(DEPTH, 6, 1, D_MODEL)
    h = normmod(d, s, g6_all[0], modv_all[0], 0, 0)
    for l in range(DEPTH):
        last = l == DEPTH - 1
        modv = modv_all[l]
        g6 = g6_all[l]

        y = matmul(swiglu_up(h, w1_bf, (l, 0)), w2_bf, (l, 0), D_MODEL, 1024, BF16)
        s, h = resid_normmod(d, s, y, g6, modv, 1, 2, FFN_RESIDUAL, 2, 3)

        z = matmul(h, w_in_bf, (l,), Z_COLS, Z_TN, F32, out_col_block=_z_col_block)
        br = hgrn_branch(d, z, hgrn_lb_logits, hgrn_norm_g[l], l, gla_consts)
        br = attention_branch(d, z, br, qk_norm_g[l], tables)
        br = conv_branches(d, z, br, short_conv_w[l], conf_dw_w[l], conf_dw_b[l], conf_ln_g[l], conf_ln_b[l])
        y = matmul(gated_merge(h, br, w_in_bf, w_branch_bf, l), w_out_bf, (l,), D_MODEL, 1024, BF16)

        if last:
            s, h = resid_normmod_latent(d, s, y, g6, modv, 3, 5, 1.0, 4, 6)
        else:
            s, h = resid_normmod(d, s, y, g6, modv, 3, 5, 1.0, 4, 6)
        y = matmul(swiglu_up(h, w1_bf, (l, 1)), w2_bf, (l, 1), D_MODEL, 1024, BF16)
        if last:
            s = resid_last_latent(d, s, y, g6, modv, 5, 8, FFN_RESIDUAL)
        else:
            s, h = resid_normmod(d, s, y, g6, modv, 5, 8, FFN_RESIDUAL, 0, 0, g6_all[l + 1], modv_all[l + 1])
    return s.reshape(d.nb, d.t_lat, D_MODEL)
```

```python
import functools
from typing import NamedTuple

import numpy as np
import jax
import jax.numpy as jnp
from jax import lax
from jax.experimental import pallas as pl
from jax.experimental.pallas import tpu as pltpu

D_MODEL = 4096
DEPTH = 2
GRID_W = 64
N_BRANCH = 4
BRANCH_W = D_MODEL // 4
HGRN_DK = 128
HGRN_DV = 128
HGRN_HEADS = BRANCH_W // HGRN_DV
HGRN_CHUNK = 64
HEAD_DIM = 128
ATTN_Q_HEADS = BRANCH_W // HEAD_DIM
ATTN_KV_HEADS = ATTN_Q_HEADS // 4
ATTN_GROUP = ATTN_Q_HEADS // ATTN_KV_HEADS
ROPE_AXIS_DIM = HEAD_DIM // 2
ROPE_THETA = 10000.0
SHORT_CONV_W = 3
CONF_CONV_W = 31
FFN_DIM = D_MODEL
FFN_RESIDUAL = 0.5
N_MOD = 9
EPS = 1e-6
LB_FLOOR = 1e-30
LOG2_E = 1.4426950408889634

HK = HGRN_HEADS * HGRN_DK
HV = HGRN_HEADS * HGRN_DV
AQ = ATTN_Q_HEADS * HEAD_DIM
AKV = ATTN_KV_HEADS * HEAD_DIM
OFF_K = 2 * HK + HV
OFF_HQ = OFF_K + 2 * AKV
OFF_GATE = OFF_HQ + HK + HV + AQ + 3 * BRANCH_W + 2 * BRANCH_W
N_IN_COLS = OFF_GATE + N_BRANCH * D_MODEL

ZF, ZB, ZI, ZHQ, ZHG, ZAQ, ZSB, ZSC, ZSU, ZGA, ZGG = (k * BRANCH_W for k in range(11))
ZK = 11 * BRANCH_W
ZV = ZK + AKV
Z_COLS = ZV + AKV
assert Z_COLS == OFF_GATE

MOD_ROWS = 16
HALO = 16
assert CONF_CONV_W // 2 < HALO and HALO % 8 == 0
LANES = 128
MXU_WIDTH = 256

V7X_VMEM_BYTES = 64 * 1024 * 1024
VMEM_LIMIT = V7X_VMEM_BYTES - 8 * 1024 * 1024

BF16 = jnp.bfloat16
F32 = jnp.float32


class Dims(NamedTuple):
    nb: int
    t_lat: int
    t_ctx: int

    @property
    def pb(self):
        return self.t_lat + self.t_ctx

    @property
    def m_all(self):
        return self.nb * self.pb

    @property
    def tiles_per_batch(self):
        return self.pb // self.t_ctx


def _params(sem):
    return pltpu.CompilerParams(dimension_semantics=sem, vmem_limit_bytes=VMEM_LIMIT)


def _mod_kernel(cc_ref, w_ref, b_ref, o_ref):
    a = cc_ref[...]
    a = (a * jax.nn.sigmoid(a)).astype(BF16)
    o_ref[...] = jnp.dot(a, w_ref[...].astype(BF16), preferred_element_type=F32) + b_ref[...]


def mod_vectors(cc, w_mod, b_mod):
    tn = 512
    n = N_MOD * D_MODEL
    return pl.pallas_call(
        _mod_kernel,
        grid=(DEPTH, n // tn),
        in_specs=[pl.BlockSpec((MOD_ROWS, D_MODEL), lambda l, j: (0, 0)),
                  pl.BlockSpec((None, D_MODEL, tn), lambda l, j: (l, 0, j)),
                  pl.BlockSpec((None, 1, tn), lambda l, j: (l, 0, j))],
        out_specs=pl.BlockSpec((None, MOD_ROWS, tn), lambda l, j: (l, 0, j)),
        out_shape=jax.ShapeDtypeStruct((DEPTH, MOD_ROWS, n), F32),
        compiler_params=_params(("arbitrary", "arbitrary")),
        name="mod_vectors",
    )(cc, w_mod, b_mod.reshape(DEPTH, 1, n))


def _mod_spec(d, chunk):
    tpb = d.tiles_per_batch
    return pl.BlockSpec((None, 1, D_MODEL), lambda i: (jnp.where(i % tpb == 0, d.nb, i // tpb), 0, chunk))


def _g_spec(k):
    return pl.BlockSpec((None, 1, D_MODEL), lambda i: (k, 0, 0))


def _row_spec(d):
    return pl.BlockSpec((d.t_ctx, D_MODEL), lambda i: (i, 0))


def _lat_in_spec(d, chunk=None):
    tpb = d.tiles_per_batch
    if chunk is None:
        return pl.BlockSpec((d.t_ctx, D_MODEL), lambda b, j: (b * tpb + 1 + j, 0))
    return pl.BlockSpec((None, 1, D_MODEL), lambda b, j: (b, 0, chunk))


def _rms(x):
    return x * lax.rsqrt(jnp.mean(x * x, axis=-1, keepdims=True) + EPS)


def _normmod_kernel(s_ref, g_ref, sh_ref, sc_ref, o_ref):
    y = _rms(s_ref[...]) * g_ref[...]
    o_ref[...] = (y * (1 + sc_ref[...]) + sh_ref[...]).astype(o_ref.dtype)


def normmod(d, s, g6, modv, g_idx, shift_idx):
    return pl.pallas_call(
        _normmod_kernel,
        grid=(d.m_all // d.t_ctx,),
        in_specs=[_row_spec(d), _g_spec(g_idx), _mod_spec(d, shift_idx), _mod_spec(d, shift_idx + 1)],
        out_specs=_row_spec(d),
        out_shape=jax.ShapeDtypeStruct((d.m_all, D_MODEL), BF16),
        compiler_params=_params(("arbitrary",)),
        name="normmod",
    )(s, g6, modv, modv)


def _resid_kernel(s_ref, y_ref, gpost_ref, gate_ref, gpre_ref, sh_ref, sc_ref, so_ref, ho_ref, *, coef):
    yn = _rms(y_ref[...].astype(F32)) * gpost_ref[...]
    s = s_ref[...] + (coef * gate_ref[...]) * yn
    so_ref[...] = s
    h = _rms(s) * gpre_ref[...]
    ho_ref[...] = (h * (1 + sc_ref[...]) + sh_ref[...]).astype(ho_ref.dtype)


def _resid_last_kernel(s_ref, y_ref, gpost_ref, gate_ref, so_ref, *, coef):
    yn = _rms(y_ref[...].astype(F32)) * gpost_ref[...]
    so_ref[...] = s_ref[...] + (coef * gate_ref[...]) * yn


def resid_normmod(d, s, y, g6, modv, gpost_idx, gate_idx, coef, gpre_idx, shift_idx, g6_pre=None, modv_pre=None):
    g6_pre = g6 if g6_pre is None else g6_pre
    modv_pre = modv if modv_pre is None else modv_pre
    return pl.pallas_call(
        functools.partial(_resid_kernel, coef=coef),
        grid=(d.m_all // d.t_ctx,),
        in_specs=[_row_spec(d), _row_spec(d), _g_spec(gpost_idx), _mod_spec(d, gate_idx),
                  _g_spec(gpre_idx), _mod_spec(d, shift_idx), _mod_spec(d, shift_idx + 1)],
        out_specs=[_row_spec(d), _row_spec(d)],
        out_shape=[jax.ShapeDtypeStruct((d.m_all, D_MODEL), F32), jax.ShapeDtypeStruct((d.m_all, D_MODEL), BF16)],
        compiler_params=_params(("arbitrary",)),
        name="resid_normmod",
    )(s, y, g6, modv, g6_pre, modv_pre, modv_pre)


def resid_normmod_latent(d, s, y, g6, modv, gpost_idx, gate_idx, coef, gpre_idx, shift_idx):
    nl = d.t_lat // d.t_ctx
    g_spec = lambda k: pl.BlockSpec((None, 1, D_MODEL), lambda b, j: (k, 0, 0))
    dense = pl.BlockSpec((d.t_ctx, D_MODEL), lambda b, j: (b * nl + j, 0))
    return pl.pallas_call(
        functools.partial(_resid_kernel, coef=coef),
        grid=(d.nb, nl),
        in_specs=[_lat_in_spec(d), _lat_in_spec(d), g_spec(gpost_idx), _lat_in_spec(d, gate_idx),
                  g_spec(gpre_idx), _lat_in_spec(d, shift_idx), _lat_in_spec(d, shift_idx + 1)],
        out_specs=[dense, dense],
        out_shape=[jax.ShapeDtypeStruct((d.nb * d.t_lat, D_MODEL), F32),
                   jax.ShapeDtypeStruct((d.nb * d.t_lat, D_MODEL), BF16)],
        compiler_params=_params(("arbitrary", "arbitrary")),
        name="resid_normmod_latent",
    )(s, y, g6, modv, g6, modv, modv)


def resid_last_latent(d, s, y, g6, modv, gpost_idx, gate_idx, coef):
    nl = d.t_lat // d.t_ctx
    dense = pl.BlockSpec((d.t_ctx, D_MODEL), lambda b, j: (b * nl + j, 0))
    return pl.pallas_call(
        functools.partial(_resid_last_kernel, coef=coef),
        grid=(d.nb, nl),
        in_specs=[dense, dense, pl.BlockSpec((None, 1, D_MODEL), lambda b, j: (gpost_idx, 0, 0)),
                  pl.BlockSpec((None, 1, D_MODEL), lambda b, j: (b, 0, gate_idx))],
        out_specs=dense,
        out_shape=jax.ShapeDtypeStruct((d.nb * d.t_lat, D_MODEL), F32),
        compiler_params=_params(("arbitrary", "arbitrary")),
        name="resid_last_latent",
    )(s, y, g6, modv)


def _row_tile(m):
    return next(t for t in (1024, 512, 256, 128) if m % t == 0)


def _mm_kernel(x_ref, w_ref, o_ref):
    o_ref[...] = jnp.dot(x_ref[...], w_ref[...], preferred_element_type=F32).astype(o_ref.dtype)


def _w_spec(lead, k, tn, col):
    return pl.BlockSpec((None,) * len(lead) + (k, tn), lambda *g: tuple(lead) + (0, col(*g)))


def matmul(x, w, lead, ncols, tn, out_dtype, out_col_block=lambda j: j):
    m, k = x.shape
    tm = _row_tile(m)
    return pl.pallas_call(
        _mm_kernel,
        grid=(m // tm, ncols // tn),
        in_specs=[pl.BlockSpec((tm, k), lambda i, j: (i, 0)),
                  _w_spec(lead, k, tn, lambda i, j: j)],
        out_specs=pl.BlockSpec((tm, tn), lambda i, j: (i, out_col_block(j))),
        out_shape=jax.ShapeDtypeStruct((m, ncols), out_dtype),
        compiler_params=_params(("arbitrary", "arbitrary")),
        name="matmul",
    )(x, w)


def _col_halves(n):
    half = n // 2
    return (slice(0, half), slice(half, n)) if half % MXU_WIDTH == 0 else (slice(0, n),)


def _swiglu_kernel(x_ref, wa_ref, wb_ref, o_ref):
    for cs in _col_halves(o_ref.shape[1]):
        a = jnp.dot(x_ref[...], wa_ref[:, cs], preferred_element_type=F32)
        b = jnp.dot(x_ref[...], wb_ref[:, cs], preferred_element_type=F32)
        o_ref[:, cs] = (a * jax.nn.sigmoid(a) * b).astype(o_ref.dtype)


def swiglu_up(h, w1, lead):
    m = h.shape[0]
    tm = _row_tile(m)
    tn = 512
    nb = FFN_DIM // tn
    return pl.pallas_call(
        _swiglu_kernel,
        grid=(m // tm, nb),
        in_specs=[pl.BlockSpec((tm, D_MODEL), lambda i, j: (i, 0)),
                  _w_spec(lead, D_MODEL, tn, lambda i, j: j),
                  _w_spec(lead, D_MODEL, tn, lambda i, j: j + nb)],
        out_specs=pl.BlockSpec((tm, tn), lambda i, j: (i, j)),
        out_shape=jax.ShapeDtypeStruct((m, FFN_DIM), BF16),
        compiler_params=_params(("arbitrary", "arbitrary")),
        name="swiglu_up",
    )(h, w1, w1)


def _merge_kernel(h_ref, br_ref, wg_ref, wb_ref, o_ref, acc_ref):
    b = pl.program_id(2)

    @pl.when(b == 0)
    def _():
        acc_ref[...] = jnp.zeros_like(acc_ref)

    for cs in _col_halves(o_ref.shape[1]):
        g = jnp.dot(h_ref[...], wg_ref[:, cs], preferred_element_type=F32)
        p = jnp.dot(br_ref[...], wb_ref[:, cs], preferred_element_type=F32)
        acc_ref[:, cs] += jax.nn.sigmoid(g) * p

    @pl.when(b == N_BRANCH - 1)
    def _():
        o_ref[...] = acc_ref[...].astype(o_ref.dtype)


def gated_merge(h, br, w_in, w_branch, layer):
    m = h.shape[0]
    tm = _row_tile(m)
    tn = 512
    nb = D_MODEL // tn
    gate0 = OFF_GATE // tn
    return pl.pallas_call(
        _merge_kernel,
        grid=(m // tm, nb, N_BRANCH),
        in_specs=[pl.BlockSpec((tm, D_MODEL), lambda i, j, b: (i, 0)),
                  pl.BlockSpec((tm, BRANCH_W), lambda i, j, b: (i, b)),
                  _w_spec((layer,), D_MODEL, tn, lambda i, j, b: gate0 + b * nb + j),
                  pl.BlockSpec((None, None, BRANCH_W, tn), lambda i, j, b: (layer, b, 0, j))],
        out_specs=pl.BlockSpec((tm, tn), lambda i, j, b: (i, j)),
        out_shape=jax.ShapeDtypeStruct((m, D_MODEL), BF16),
        scratch_shapes=[pltpu.VMEM((tm, tn), F32)],
        compiler_params=_params(("arbitrary", "arbitrary", "arbitrary")),
        name="gated_merge",
    )(h, br, w_in, w_branch)


_GLA_LEVELS = (8, 16, 32)
GLA_HEADS_PER_STEP = 2


def _gla_constants():
    c = HGRN_CHUNK
    i = np.arange(c)[:, None]
    t = np.arange(c)[None, :]
    flip = lambda m: m[::-1, ::-1]
    mats = [t <= i]
    masks = []
    for s in _GLA_LEVELS:
        blk = i // s
        right = blk % 2 == 1
        mats.append(right & (t > blk * s) & (t <= i))
        mats.append(~right & (t > i) & (t <= (blk + 1) * s))
        masks.append(right & (t // s == blk - 1))
    mats.append(t > i)
    g_f = np.concatenate(mats, 0).astype(np.float32)
    g_b = np.concatenate([flip(m) for m in mats], 0).astype(np.float32)
    m_f = np.stack(masks).astype(np.float32)
    m_b = np.stack([flip(m) for m in masks]).astype(np.float32)
    tile3 = lambda g: jnp.asarray(np.tile(g, (1, 3)), BF16)
    return tile3(g_f), tile3(g_b), jnp.asarray(m_f), jnp.asarray(m_b)


def _nt_dot(a, b):
    return lax.dot_general(a, b, (((1,), (1,)), ((), ())), preferred_element_type=F32)


def _gla_gates(a, lb_floor, one_minus_lb, gmat):
    f = lb_floor + one_minus_lb * jax.nn.sigmoid(a)
    lf = jnp.log(f)
    hi = lf.astype(BF16)
    r1 = lf - hi.astype(F32)
    mid = r1.astype(BF16)
    lo = (r1 - mid.astype(F32)).astype(BF16)
    ex = jnp.dot(gmat, jnp.concatenate([hi, mid, lo], axis=0), preferred_element_type=F32)
    return 1.0 - f, ex


def _gla_direct_terms(q, k, cum, reverse):
    c = HGRN_CHUNK
    nsub = c // 8
    q3 = q.reshape(nsub, 8, LANES)
    k3 = k.reshape(nsub, 8, LANES)
    c3 = cum.reshape(nsub, 8, LANES)
    ii = lax.broadcasted_iota(jnp.int32, (nsub, 8, LANES), 1)
    ws = []
    for jj in range(8):
        kj = jnp.broadcast_to(k3[:, jj:jj + 1, :], (nsub, 8, LANES))
        cj = jnp.broadcast_to(c3[:, jj:jj + 1, :], (nsub, 8, LANES))
        w = q3 * kj * jnp.exp(jnp.minimum(c3 - cj, 0.0))
        keep = (ii <= jj) if reverse else (ii >= jj)
        ws.append(jnp.where(keep, w, 0.0).reshape(c, LANES).astype(BF16))
    return jnp.concatenate(ws, axis=0)


def _gla_chunk(q, k, v, ex, masks, st, reverse):
    c = HGRN_CHUNK
    nsub = c // 8
    g_rows = (2 + 2 * len(_GLA_LEVELS)) * c
    cum = ex[0:c]
    vb16 = v.astype(BF16)

    p = jnp.zeros((c, c), F32)
    for li in range(len(_GLA_LEVELS)):
        ea = ex[c * (1 + 2 * li):c * (2 + 2 * li)]
        eb = ex[c * (2 + 2 * li):c * (3 + 2 * li)]
        qa = (q * jnp.exp(jnp.minimum(ea, 0.0))).astype(BF16)
        kb = (k * jnp.exp(jnp.minimum(eb, 0.0))).astype(BF16)
        p = p + masks[li] * _nt_dot(qa, kb)
    o = jnp.dot(p.astype(BF16), vb16, preferred_element_type=F32)

    rs = jnp.dot(_gla_direct_terms(q, k, cum, reverse), jnp.ones((LANES, LANES), BF16), preferred_element_type=F32)
    v3 = v.reshape(nsub, 8, LANES)
    o3 = o.reshape(nsub, 8, LANES)
    for jj in range(8):
        vj = jnp.broadcast_to(v3[:, jj:jj + 1, :], (nsub, 8, LANES))
        o3 = o3 + rs[jj * c:(jj + 1) * c].reshape(nsub, 8, LANES) * vj
    o = o3.reshape(c, LANES)

    o = o + _nt_dot((q * jnp.exp(cum)).astype(BF16), st.astype(BF16))
    ke = (k * jnp.exp(ex[g_rows - c:g_rows])).astype(BF16)
    total = cum[0:1] if reverse else cum[c - 1:c]
    st_new = jnp.exp(total) * st + jnp.dot(v.T.astype(BF16), ke, preferred_element_type=F32)
    return o, st_new


def _gla_kernel(af_ref, ab_ref, v_ref, q_ref, g_ref, lbl_ref, ng_ref, gf_ref, gb_ref, mf_ref, mb_ref,
                o_ref, of_ref, ob_ref, sf_ref, sb_ref, *, layer, t_ctx, t_lat):
    c = HGRN_CHUNK
    logits = lbl_ref[...]
    e = jnp.exp(logits - jnp.max(logits, axis=0, keepdims=True))
    pr = e / jnp.sum(e, axis=0, keepdims=True)
    csum = pr[0]
    for dd in range(1, layer + 1):
        csum = csum + pr[dd]
    lb = csum - pr[0]
    lb_floor = jnp.maximum(lb, LB_FLOOR)
    one_minus_lb = 1.0 - lb
    masks_f = [mf_ref[li] for li in range(len(_GLA_LEVELS))]
    masks_b = [mb_ref[li] for li in range(len(_GLA_LEVELS))]
    gmat_f = gf_ref[...]
    gmat_b = gb_ref[...]

    sf_ref[...] = jnp.zeros_like(sf_ref)
    sb_ref[...] = jnp.zeros_like(sb_ref)

    def scan(row0, nchunks):
        def body(ci, carry):
            rf = pl.multiple_of(row0 + ci * c, c)
            rb = pl.multiple_of(row0 + (nchunks - 1 - ci) * c, c)
            lanes = [slice(hh * LANES, (hh + 1) * LANES) for hh in range(GLA_HEADS_PER_STEP)]
            q_f, v_f = q_ref[pl.ds(rf, c), :], v_ref[pl.ds(rf, c), :]
            q_b, v_b = q_ref[pl.ds(rb, c), :], v_ref[pl.ds(rb, c), :]
            st_f = [sf_ref[hh] for hh in range(GLA_HEADS_PER_STEP)]
            st_b = [sb_ref[hh] for hh in range(GLA_HEADS_PER_STEP)]
            k_f, ex_f = _gla_gates(af_ref[pl.ds(rf, c), :], lb_floor[0:1], one_minus_lb[0:1], gmat_f)
            k_b, ex_b = _gla_gates(ab_ref[pl.ds(rb, c), :], lb_floor[1:2], one_minus_lb[1:2], gmat_b)
            outs = []
            for hh, ln in enumerate(lanes):
                outs.append(_gla_chunk(q_f[:, ln], k_f[:, ln], v_f[:, ln], ex_f[:, ln], masks_f, st_f[hh], False))
                outs.append(_gla_chunk(q_b[:, ln], k_b[:, ln], v_b[:, ln], ex_b[:, ln], masks_b, st_b[hh], True))
            for hh, ln in enumerate(lanes):
                (o_f, s_f), (o_b, s_b) = outs[2 * hh], outs[2 * hh + 1]
                of_ref[pl.ds(rf, c), ln] = o_f
                ob_ref[pl.ds(rb, c), ln] = o_b
                sf_ref[hh] = s_f
                sb_ref[hh] = s_b
            return carry
        lax.fori_loop(0, nchunks, body, 0, unroll=4)

    scan(0, t_ctx // c)
    scan(t_ctx, t_lat // c)

    for hh in range(GLA_HEADS_PER_STEP):
        ln = slice(hh * LANES, (hh + 1) * LANES)
        o = of_ref[:, ln] + ob_ref[:, ln]
        y = _rms(o) * ng_ref[...] * jax.nn.sigmoid(g_ref[:, ln])
        o_ref[:, ln] = y.astype(o_ref.dtype)


def hgrn_branch(d, z, lb_logits, norm_g, layer, consts):
    g_f, g_b, m_f, m_b = consts
    w = GLA_HEADS_PER_STEP * LANES
    col = lambda base: (lambda b, h: (b, base // w + h))
    zspec = lambda base: pl.BlockSpec((d.pb, w), col(base))
    const2 = lambda shape: pl.BlockSpec(shape, lambda b, h: (0,) * len(shape))
    state = pltpu.VMEM((GLA_HEADS_PER_STEP, HGRN_DV, HGRN_DK), F32)
    return pl.pallas_call(
        functools.partial(_gla_kernel, layer=layer, t_ctx=d.t_ctx, t_lat=d.t_lat),
        grid=(d.nb, HGRN_HEADS // GLA_HEADS_PER_STEP),
        in_specs=[zspec(ZF), zspec(ZB), zspec(ZI), zspec(ZHQ), zspec(ZHG),
                  pl.BlockSpec((DEPTH, 2, w), lambda b, h: (0, 0, h)),
                  const2((1, HGRN_DV)), const2(g_f.shape), const2(g_b.shape), const2(m_f.shape), const2(m_b.shape)],
        out_specs=pl.BlockSpec((d.pb, w), lambda b, h: (b, h)),
        out_shape=jax.ShapeDtypeStruct((d.m_all, D_MODEL), BF16),
        scratch_shapes=[pltpu.VMEM((d.pb, w), F32), pltpu.VMEM((d.pb, w), F32), state, state],
        compiler_params=_params(("arbitrary", "arbitrary")),
        name="hgrn_branch",
    )(z, z, z, z, z, lb_logits, norm_g.reshape(1, HGRN_DV), g_f, g_b, m_f, m_b)


def _rope_tables(d):
    half = ROPE_AXIS_DIM // 2
    tok = np.arange(d.t_lat)
    inv = ROPE_THETA ** (-np.arange(0, ROPE_AXIS_DIM, 2, dtype=np.float32) / ROPE_AXIS_DIM)
    ang_r = (tok // GRID_W).astype(np.float32)[:, None] * inv.astype(np.float32)
    ang_c = (tok % GRID_W).astype(np.float32)[:, None] * inv.astype(np.float32)
    ang = jnp.asarray(np.concatenate([ang_r, ang_r, ang_c, ang_c], axis=1), F32)
    first = jnp.asarray((np.arange(HEAD_DIM) % ROPE_AXIS_DIM) < half)
    cos = jnp.cos(ang)
    sin = jnp.sin(ang)
    sin_a = jnp.where(first, -sin, 0.0)
    sin_b = jnp.where(first, 0.0, sin)
    pad = lambda t, fill: jnp.concatenate([jnp.full((d.t_ctx, HEAD_DIM), fill, F32), t], axis=0)
    return pad(cos, 1.0), pad(sin_a, 0.0), pad(sin_b, 0.0)


def _rope(u, cos, sin_a, sin_b):
    half = ROPE_AXIS_DIM // 2
    return u * cos + pltpu.roll(u, HEAD_DIM - half, 1) * sin_a + pltpu.roll(u, half, 1) * sin_b


def _attn_kernel(q_ref, k_ref, v_ref, g_ref, ck_ref, sak_ref, sbk_ref, cq_ref, saq_ref, sbq_ref, br_ref,
                 o_ref, kbuf, vbuf, *, t_ctx):
    del br_ref
    qi = pl.program_id(2)
    tq = q_ref.shape[0]
    scale = HEAD_DIM ** -0.5

    @pl.when(qi == 0)
    def _():
        kn = _rms(k_ref[...]) * g_ref[1:2, :]
        kbuf[...] = _rope(kn, ck_ref[...], sak_ref[...], sbk_ref[...]).astype(BF16)
        vbuf[...] = v_ref[...].astype(BF16)

    q = q_ref[...]
    cq, saq, sbq = cq_ref[...], saq_ref[...], sbq_ref[...]
    qs = []
    for gi in range(ATTN_GROUP):
        qn = _rms(q[:, gi * HEAD_DIM:(gi + 1) * HEAD_DIM]) * g_ref[0:1, :]
        qs.append(_rope(qn, cq, saq, sbq).astype(BF16))
    pair = 1
    stacks = [jnp.concatenate(qs[g0:g0 + pair], axis=0) for g0 in range(0, ATTN_GROUP, pair)]

    def attend(keys, vals):
        for si, qst in enumerate(stacks):
            s = _nt_dot(qst, keys)
            m = jnp.max(s, axis=-1, keepdims=True)
            p = jnp.exp2((s - m) * (scale * LOG2_E))
            l = jnp.sum(p, axis=-1, keepdims=True)
            o = jnp.dot(p.astype(BF16), vals, preferred_element_type=F32) / l
            for gj in range(pair):
                gi = si * pair + gj
                o_ref[:, gi * HEAD_DIM:(gi + 1) * HEAD_DIM] = o[gj * tq:(gj + 1) * tq].astype(o_ref.dtype)

    @pl.when(qi == 0)
    def _():
        attend(kbuf[0:t_ctx, :], vbuf[0:t_ctx, :])

    @pl.when(qi > 0)
    def _():
        attend(kbuf[...], vbuf[...])


def attention_branch(d, z, br, qk_norm_g, tables):
    cos, sin_a, sin_b = tables
    tq = d.t_ctx
    nq = d.tiles_per_batch
    gw = ATTN_GROUP * HEAD_DIM
    kspec = lambda base: pl.BlockSpec((d.pb, HEAD_DIM), lambda b, h, qi: (b, base // HEAD_DIM + h))
    tab_k = pl.BlockSpec((d.pb, HEAD_DIM), lambda b, h, qi: (0, 0))
    tab_q = pl.BlockSpec((tq, HEAD_DIM), lambda b, h, qi: (qi, 0))
    return pl.pallas_call(
        functools.partial(_attn_kernel, t_ctx=d.t_ctx),
        grid=(d.nb, ATTN_KV_HEADS, nq),
        in_specs=[pl.BlockSpec((tq, gw), lambda b, h, qi: (b * nq + qi, ZAQ // gw + h)),
                  kspec(ZK), kspec(ZV),
                  pl.BlockSpec((2, HEAD_DIM), lambda b, h, qi: (0, 0)),
                  tab_k, tab_k, tab_k, tab_q, tab_q, tab_q,
                  pl.BlockSpec(memory_space=pl.ANY)],
        out_specs=pl.BlockSpec((tq, gw), lambda b, h, qi: (b * nq + qi, BRANCH_W // gw + h)),
        out_shape=jax.ShapeDtypeStruct((d.m_all, D_MODEL), BF16),
        scratch_shapes=[pltpu.VMEM((d.pb, HEAD_DIM), BF16), pltpu.VMEM((d.pb, HEAD_DIM), BF16)],
        input_output_aliases={10: 0},
        compiler_params=_params(("arbitrary", "arbitrary", "arbitrary")),
        name="attention_branch",
    )(z, z, z, qk_norm_g, cos, sin_a, sin_b, cos, sin_a, sin_b, br)


def _conv_kernel(sb_ref, sc_ref, su_ref, ga_ref, gg_ref,
                 scp_ref, sup_ref, gap_ref, ggp_ref, scn_ref, sun_ref, gan_ref, ggn_ref,
                 ws_ref, wc_ref, bc_ref, lg_ref, lb_ref, br_ref, o_ref, ps_ref, pc_ref, pcs_ref, *, tiles_per_batch):
    del br_ref
    tt = sb_ref.shape[0]
    within = pl.program_id(0) % tiles_per_batch
    prev_ok = (within >= 2).astype(F32)
    next_ok = jnp.logical_and(within >= 1, within <= tiles_per_batch - 2).astype(F32)

    glu = lambda a, g: a * jax.nn.sigmoid(g)
    ps_ref[0:HALO, :] = scp_ref[...] * sup_ref[...] * prev_ok
    ps_ref[HALO:HALO + tt, :] = sc_ref[...] * su_ref[...]
    ps_ref[HALO + tt:2 * HALO + tt, :] = scn_ref[...] * sun_ref[...] * next_ok
    pc_ref[0:HALO, :] = glu(gap_ref[...], ggp_ref[...]) * prev_ok
    pc_ref[HALO:HALO + tt, :] = glu(ga_ref[...], gg_ref[...])
    pc_ref[HALO + tt:2 * HALO + tt, :] = glu(gan_ref[...], ggn_ref[...]) * next_ok

    acc = jnp.zeros((tt, BRANCH_W), F32)
    for tau in range(SHORT_CONV_W):
        acc = acc + ps_ref[pl.ds(HALO - SHORT_CONV_W // 2 + tau, tt), :] * ws_ref[tau:tau + 1, :]
    o_ref[:, 0:BRANCH_W] = (sb_ref[...] * acc).astype(o_ref.dtype)

    n = tt + 2 * HALO - 8
    for r in range(1, 8):
        pcs_ref[r - 1] = pc_ref[pl.ds(r, n), :]
    acc = jnp.zeros((tt, BRANCH_W), F32)
    for tau in range(CONF_CONV_W):
        off = HALO - CONF_CONV_W // 2 + tau
        base = (off // 8) * 8
        tap = pc_ref[pl.ds(base, tt), :] if off % 8 == 0 else pcs_ref[off % 8 - 1, pl.ds(base, tt), :]
        acc = acc + tap * wc_ref[tau:tau + 1, :]
    u = acc + bc_ref[...]
    uc = u - jnp.mean(u, axis=-1, keepdims=True)
    y = uc * lax.rsqrt(jnp.mean(uc * uc, axis=-1, keepdims=True) + EPS) * lg_ref[...] + lb_ref[...]
    o_ref[:, BRANCH_W:2 * BRANCH_W] = (y * jax.nn.sigmoid(y)).astype(o_ref.dtype)


def conv_branches(d, z, br, short_w, dw_w, dw_b, ln_g, ln_b):
    tt = d.t_ctx
    per = tt // HALO
    last = d.m_all // HALO - 1
    cur = lambda base: pl.BlockSpec((tt, BRANCH_W), lambda i: (i, base // BRANCH_W))
    prv = lambda base: pl.BlockSpec((HALO, BRANCH_W), lambda i: (jnp.maximum(i * per - 1, 0), base // BRANCH_W))
    nxt = lambda base: pl.BlockSpec((HALO, BRANCH_W), lambda i: (jnp.minimum((i + 1) * per, last), base // BRANCH_W))
    full = lambda shape: pl.BlockSpec(shape, lambda i: (0,) * len(shape))
    row = full((1, BRANCH_W))
    return pl.pallas_call(
        functools.partial(_conv_kernel, tiles_per_batch=d.tiles_per_batch),
        grid=(d.m_all // tt,),
        in_specs=[cur(ZSB), cur(ZSC), cur(ZSU), cur(ZGA), cur(ZGG),
                  prv(ZSC), prv(ZSU), prv(ZGA), prv(ZGG), nxt(ZSC), nxt(ZSU), nxt(ZGA), nxt(ZGG),
                  full((SHORT_CONV_W, BRANCH_W)), full((CONF_CONV_W, BRANCH_W)), row, row, row,
                  pl.BlockSpec(memory_space=pl.ANY)],
        out_specs=pl.BlockSpec((tt, 2 * BRANCH_W), lambda i: (i, 1)),
        out_shape=jax.ShapeDtypeStruct((d.m_all, D_MODEL), BF16),
        scratch_shapes=[pltpu.VMEM((tt + 2 * HALO, BRANCH_W), F32), pltpu.VMEM((tt + 2 * HALO, BRANCH_W), F32),
                        pltpu.VMEM((7, tt + 2 * HALO - 8, BRANCH_W), F32)],
        input_output_aliases={18: 0},
        compiler_params=_params(("arbitrary",)),
        name="conv_branches",
    )(z, z, z, z, z, z, z, z, z, z, z, z, z, short_w, dw_w, dw_b.reshape(1, BRANCH_W),
      ln_g.reshape(1, BRANCH_W), ln_b.reshape(1, BRANCH_W), br)


Z_TN = 2 * AKV
assert OFF_K % Z_TN == 0 and OFF_HQ - OFF_K == Z_TN and Z_COLS % Z_TN == 0


def _z_col_block(j):
    kv = OFF_K // Z_TN
    return jnp.where(j < kv, j, jnp.where(j == kv, Z_COLS // Z_TN - 1, j - 1))


def kernel(x, c, ctx, c_ctx, w_mod, b_mod, norm_g, ffn_w1, ffn_w2, w_in, hgrn_lb_logits, hgrn_norm_g, qk_norm_g,
           short_conv_w, conf_dw_w, conf_dw_b, conf_ln_g, conf_ln_b, w_branch, w_out):
    d = Dims(nb=x.shape[0], t_lat=x.shape[1], t_ctx=ctx.shape[1])
    assert d.nb + 1 <= MOD_ROWS and d.t_lat % d.t_ctx == 0 and d.t_ctx % HGRN_CHUNK == 0 and d.t_ctx % HALO == 0
    assert d.t_lat % GRID_W == 0 and d.t_ctx >= HALO

    cc = jnp.concatenate([c, c_ctx[None], jnp.zeros((MOD_ROWS - d.nb - 1, D_MODEL), F32)], axis=0)
    modv_all = mod_vectors(cc, w_mod, b_mod).reshape(DEPTH, MOD_ROWS, 1, N_MOD * D_MODEL)

    s = jnp.concatenate([ctx, x], axis=1).reshape(d.m_all, D_MODEL)
    w1_bf = ffn_w1.astype(BF16)
    w2_bf = ffn_w2.astype(BF16)
    w_in_bf = w_in.astype(BF16)
    w_branch_bf = w_branch.astype(BF16)
    w_out_bf = w_out.astype(BF16)
    gla_consts = _gla_constants()
    tables = _rope_tables(d)

    g6_all = norm_g.reshape(DEPTH, 6, 1, D_MODEL)
    h = normmod(d, s, g6_all[0], modv_all[0], 0, 0)
    for l in range(DEPTH):
        last = l == DEPTH - 1
        modv = modv_all[l]
        g6 = g6_all[l]

        y = matmul(swiglu_up(h, w1_bf, (l, 0)), w2_bf, (l, 0), D_MODEL, 1024, BF16)
        s, h = resid_normmod(d, s, y, g6, modv, 1, 2, FFN_RESIDUAL, 2, 3)

        z = matmul(h, w_in_bf, (l,), Z_COLS, Z_TN, F32, out_col_block=_z_col_block)
        br = hgrn_branch(d, z, hgrn_lb_logits, hgrn_norm_g[l], l, gla_consts)
        br = attention_branch(d, z, br, qk_norm_g[l], tables)
        br = conv_branches(d, z, br, short_conv_w[l], conf_dw_w[l], conf_dw_b[l], conf_ln_g[l], conf_ln_b[l])
        y = matmul(gated_merge(h, br, w_in_bf, w_branch_bf, l), w_out_bf, (l,), D_MODEL, 1024, BF16)

        if last:
            s, h = resid_normmod_latent(d, s, y, g6, modv, 3, 5, 1.0, 4, 6)
        else:
            s, h = resid_normmod(d, s, y, g6, modv, 3, 5, 1.0, 4, 6)
        y = matmul(swiglu_up(h, w1_bf, (l, 1)), w2_bf, (l, 1), D_MODEL, 1024, BF16)
        if last:
            s = resid_last_latent(d, s, y, g6, modv, 5, 8, FFN_RESIDUAL)
        else:
            s, h = resid_normmod(d, s, y, g6, modv, 5, 8, FFN_RESIDUAL, 0, 0, g6_all[l + 1], modv_all[l + 1])
    return s.reshape(d.nb, d.t_lat, D_MODEL)
```
